```python
import math
import jax, jax.numpy as jnp
from jax import lax
import numpy as np

D_MODEL = 1024
BATCH = 16
SEQ = 2048
DEPTH = 1

D_MIX = D_MODEL
CONV_CH = D_MIX // 2
CONV_GROUPS = 8
CONV_K = 31
FOX_HEADS = 8
FOX_HEAD_DIM = 64
FOX_W = FOX_HEADS * FOX_HEAD_DIM
Q_BLOCK = 128
MEM_LEN = 256
MEM_HEADS = 4
MEM_HEAD_DIM = D_MODEL // MEM_HEADS
D_FF = ((8 * D_MODEL // 3 + 255) // 256) * 256
EPS = 1e-6

OFF_U = 0
OFF_G = OFF_U + CONV_CH
OFF_Q = OFF_G + CONV_CH
OFF_K = OFF_Q + FOX_W
OFF_V = OFF_K + FOX_W
OFF_F = OFF_V + FOX_W
D_IN = OFF_F + FOX_HEADS

kernel_name = "hybrid_conformer_fox_memory_block"


def rmsnorm(x, g):
    xf = x.astype(jnp.float32)
    y = xf * lax.rsqrt(jnp.mean(xf * xf, axis=-1, keepdims=True) + EPS)
    return (y * g.astype(jnp.float32)).astype(x.dtype)


def layernorm(x, g, b):
    xf = x.astype(jnp.float32)
    mu = jnp.mean(xf, axis=-1, keepdims=True)
    xc = xf - mu
    y = xc * lax.rsqrt(jnp.mean(xc * xc, axis=-1, keepdims=True) + EPS)
    return (y * g.astype(jnp.float32) + b.astype(jnp.float32)).astype(x.dtype)


def conformer_conv(u, gate, conv_w, conv_b, ln_g, ln_b):
    a = u * jax.nn.sigmoid(gate)
    y = lax.conv_general_dilated(
        a, conv_w[:, None, :].astype(a.dtype),
        window_strides=(1,), padding=[(CONV_K - 1, 0)],
        dimension_numbers=("NWC", "WIO", "NWC"),
        feature_group_count=CONV_CH) + conv_b.astype(a.dtype)
    return jax.nn.silu(layernorm(y, ln_g, ln_b))


def forgetting_attention(q, k, v, logf):
    b, s, h, dh = q.shape
    scale = 1.0 / math.sqrt(dh)
    qh = jnp.transpose(q, (0, 2, 1, 3))
    kh = jnp.transpose(k, (0, 2, 1, 3))
    vh = jnp.transpose(v, (0, 2, 1, 3))
    c = jnp.transpose(jnp.cumsum(logf, axis=1), (0, 2, 1))
    outs = []
    for i in range(s // Q_BLOCK):
        q0, end = i * Q_BLOCK, (i + 1) * Q_BLOCK
        logits = jnp.einsum("bhqd,bhkd->bhqk", qh[:, :, q0:end], kh[:, :, :end],
                            preferred_element_type=jnp.float32) * scale
        logits = logits + (c[:, :, q0:end, None] - c[:, :, None, :end])
        causal = jnp.arange(end)[None, :] <= (q0 + jnp.arange(Q_BLOCK))[:, None]
        logits = jnp.where(causal[None, None], logits, -jnp.inf)
        p = jax.nn.softmax(logits, axis=-1)
        outs.append(jnp.einsum("bhqk,bhkd->bhqd", p.astype(vh.dtype), vh[:, :, :end]))
    o = jnp.concatenate(outs, axis=2)
    return jnp.transpose(o, (0, 2, 1, 3)).reshape(b, s, h * dh)


def memory_cross_attention(hx, mem_n, w_mq, w_mkv, w_mo):
    b, s, _ = hx.shape
    m = mem_n.shape[1]
    q = (hx @ w_mq).reshape(b, s, MEM_HEADS, MEM_HEAD_DIM)
    kv = mem_n @ w_mkv
    k = kv[..., :D_MODEL].reshape(b, m, MEM_HEADS, MEM_HEAD_DIM)
    v = kv[..., D_MODEL:].reshape(b, m, MEM_HEADS, MEM_HEAD_DIM)
    logits = jnp.einsum("bshd,bmhd->bhsm", q, k,
                        preferred_element_type=jnp.float32) / math.sqrt(MEM_HEAD_DIM)
    p = jax.nn.softmax(logits, axis=-1)
    o = jnp.einsum("bhsm,bmhd->bshd", p.astype(v.dtype), v).reshape(b, s, D_MODEL)
    return o @ w_mo


def setup_inputs(seed: int = 0) -> dict:
    key = jax.random.key(seed)
    ks = jax.random.split(key, 24)
    f32 = jnp.float32

    def nrm(k, shape, fan_in):
        return jax.random.normal(k, shape, f32) * (fan_in ** -0.5)

    def gain(k, shape):
        return 1.0 + 0.02 * jax.random.normal(k, shape, f32)

    def small(k, shape, s=0.02):
        return s * jax.random.normal(k, shape, f32)

    return {
        "x": jax.random.normal(ks[0], (BATCH, SEQ, D_MODEL), f32),
        "mem": jax.random.normal(ks[1], (BATCH, MEM_LEN, D_MODEL), f32),
        "g_mix": gain(ks[2], (DEPTH, D_MODEL)),
        "w_in": nrm(ks[3], (DEPTH, D_MODEL, D_IN), D_MODEL),
        "b_f": 2.0 + small(ks[4], (DEPTH, FOX_HEADS), 0.5),
        "conv_w": nrm(ks[5], (DEPTH, CONV_K, CONV_CH), CONV_K),
        "conv_b": small(ks[6], (DEPTH, CONV_CH)),
        "ln_g": gain(ks[7], (DEPTH, CONV_CH)),
        "ln_b": small(ks[8], (DEPTH, CONV_CH)),
        "w_out": nrm(ks[9], (DEPTH, D_MIX, D_MODEL), D_MIX),
        "g_x": gain(ks[10], (DEPTH, D_MODEL)),
        "g_mem": gain(ks[11], (D_MODEL,)),
        "w_mq": nrm(ks[12], (DEPTH, D_MODEL, D_MODEL), D_MODEL),
        "w_mkv": nrm(ks[13], (DEPTH, D_MODEL, 2 * D_MODEL), D_MODEL),
        "w_mo": nrm(ks[14], (DEPTH, D_MODEL, D_MODEL), D_MODEL),
        "g_ffn": gain(ks[15], (DEPTH, D_MODEL)),
        "w_gu": nrm(ks[16], (DEPTH, D_MODEL, 2 * D_FF), D_MODEL),
        "w_down": nrm(ks[17], (DEPTH, D_FF, D_MODEL), D_FF),
        "g_final": gain(ks[18], (D_MODEL,)),
    }


def reference(x, mem, g_mix, w_in, b_f, conv_w, conv_b, ln_g, ln_b, w_out,
              g_x, g_mem, w_mq, w_mkv, w_mo, g_ffn, w_gu, w_down, g_final):
    b, s, _ = x.shape
    mem_n = rmsnorm(mem, g_mem)
    for l in range(DEPTH):
        h = rmsnorm(x, g_mix[l])
        z = h @ w_in[l]
        conv_out = conformer_conv(z[..., OFF_U:OFF_G], z[..., OFF_G:OFF_Q],
                                  conv_w[l], conv_b[l], ln_g[l], ln_b[l])
        q = z[..., OFF_Q:OFF_K].reshape(b, s, FOX_HEADS, FOX_HEAD_DIM)
        k = z[..., OFF_K:OFF_V].reshape(b, s, FOX_HEADS, FOX_HEAD_DIM)
        v = z[..., OFF_V:OFF_F].reshape(b, s, FOX_HEADS, FOX_HEAD_DIM)
        logf = jax.nn.log_sigmoid((z[..., OFF_F:] + b_f[l]).astype(jnp.float32))
        att_out = forgetting_attention(q, k, v, logf)
        x = x + jnp.concatenate([conv_out, att_out], axis=-1) @ w_out[l]
        x = x + memory_cross_attention(rmsnorm(x, g_x[l]), mem_n, w_mq[l], w_mkv[l], w_mo[l])
        gu = rmsnorm(x, g_ffn[l]) @ w_gu[l]
        x = x + (jax.nn.silu(gu[..., :D_FF]) * gu[..., D_FF:]) @ w_down[l]
    return rmsnorm(x, g_final)
```

```python
import functools
import math

import jax
import jax.numpy as jnp
from jax import lax
from jax.experimental import pallas as pl
from jax.experimental.pallas import tpu as pltpu

F32 = jnp.float32
BF16 = jnp.bfloat16

EPS = 1e-6
CONV_K = 31
FOX_HEADS = 8
FOX_HEAD_DIM = 64
MEM_HEADS = 4

LANES = 128
CUM_BLOCK = 128
HALO = 32
VMEM_LIMIT = 56 * 1024 * 1024


def _params(semantics):
    return pltpu.CompilerParams(dimension_semantics=semantics, vmem_limit_bytes=VMEM_LIMIT)


def _rms(x, g):
    return x * lax.rsqrt(jnp.mean(x * x, axis=-1, keepdims=True) + EPS) * g


def _split3(x):
    hi = x.astype(BF16)
    r1 = x - hi.astype(F32)
    mid = r1.astype(BF16)
    lo = (r1 - mid.astype(F32)).astype(BF16)
    return hi, mid, lo


def _dot(a, b):
    return jnp.dot(a, b, preferred_element_type=F32)


def _dot_nt(a, b):
    return lax.dot_general(a, b, (((1,), (1,)), ((), ())), preferred_element_type=F32)


def _mem_kv_kernel(mem_ref, g_ref, w_ref, k_ref, v_ref):
    d = mem_ref.shape[-1]
    mn = _rms(mem_ref[...], g_ref[...]).astype(BF16)
    k_ref[...] = _dot(mn, w_ref[:, :d]).astype(BF16)
    v_ref[...] = _dot(mn, w_ref[:, d:]).astype(BF16)


def _mem_kv(mem, g_mem, w_mkv):
    b, m, d = mem.shape
    return pl.pallas_call(
        _mem_kv_kernel,
        grid=(b,),
        in_specs=[
            pl.BlockSpec((None, m, d), lambda i: (i, 0, 0)),
            pl.BlockSpec((1, d), lambda i: (0, 0)),
            pl.BlockSpec((d, 2 * d), lambda i: (0, 0)),
        ],
        out_specs=[
            pl.BlockSpec((None, m, d), lambda i: (i, 0, 0)),
            pl.BlockSpec((None, m, d), lambda i: (i, 0, 0)),
        ],
        out_shape=[jax.ShapeDtypeStruct((b, m, d), BF16)] * 2,
        compiler_params=_params(("parallel",)),
        name="mem_kv",
    )(mem, g_mem, w_mkv)


def _in_proj_kernel(x_ref, g_ref, w_ref, wf_ref, bf_ref,
                    a_ref, q_ref, k_ref, v_ref, cq_ref, ck_ref, carry_ref, *, cw, fw):
    @pl.when(pl.program_id(1) == 0)
    def _():
        carry_ref[...] = jnp.zeros_like(carry_ref)

    tm = x_ref.shape[0]
    h = _rms(x_ref[...], g_ref[...]).astype(BF16)

    u = _dot(h, w_ref[:, 0:cw])
    gate = _dot(h, w_ref[:, cw:2 * cw])
    a_ref[...] = u * jax.nn.sigmoid(gate)
    off = 2 * cw
    scale = 1.0 / math.sqrt(FOX_HEAD_DIM)
    q_ref[...] = (_dot(h, w_ref[:, off:off + fw]) * scale).astype(BF16)
    k_ref[...] = _dot(h, w_ref[:, off + fw:off + 2 * fw]).astype(BF16)
    v_ref[...] = _dot(h, w_ref[:, off + 2 * fw:off + 3 * fw]).astype(BF16)

    zf = _dot(h, wf_ref[...]) + bf_ref[...]
    logf = jnp.minimum(zf, 0.0) - jnp.log1p(jnp.exp(-jnp.abs(zf)))

    row = lax.broadcasted_iota(jnp.int32, (CUM_BLOCK, CUM_BLOCK), 0)
    col = lax.broadcasted_iota(jnp.int32, (CUM_BLOCK, CUM_BLOCK), 1)
    tri = (col <= row).astype(BF16)

    lane = lax.broadcasted_iota(jnp.int32, (CUM_BLOCK, LANES), 1)
    sub = lane & 7
    valid = lane < FOX_HEADS * 8
    carry = carry_ref[...]
    for r in range(tm // CUM_BLOCK):
        blk = logf[r * CUM_BLOCK:(r + 1) * CUM_BLOCK, :]
        hi, mid, lo = _split3(blk)
        c = (_dot(tri, hi) + _dot(tri, mid)) + _dot(tri, lo) + carry
        carry = c[CUM_BLOCK - 1:CUM_BLOCK, :]
        chi, cmid, clo = (t.astype(F32) for t in _split3(c))
        one = jnp.ones_like(c)
        zero = jnp.zeros_like(c)
        cq = jnp.where(sub == 0, chi, jnp.where(sub == 1, cmid, jnp.where(sub == 2, clo,
             jnp.where(sub < 6, one, zero))))
        ck = jnp.where(sub < 3, one, jnp.where(sub == 3, -chi, jnp.where(sub == 4, -cmid,
             jnp.where(sub == 5, -clo, zero))))
        cq_ref[r * CUM_BLOCK:(r + 1) * CUM_BLOCK, :] = jnp.where(valid, cq, zero).astype(BF16)
        ck_ref[r * CUM_BLOCK:(r + 1) * CUM_BLOCK, :] = jnp.where(valid, ck, zero).astype(BF16)
    carry_ref[...] = carry


def _in_proj(x, g_mix, w_main, w_f, b_f, *, tm, cw, fw):
    b, s, d = x.shape
    n_main = w_main.shape[1]
    row = lambda i, j: (i, j, 0)
    const = lambda i, j: (0, 0)
    return pl.pallas_call(
        functools.partial(_in_proj_kernel, cw=cw, fw=fw),
        grid=(b, s // tm),
        in_specs=[
            pl.BlockSpec((None, tm, d), row),
            pl.BlockSpec((1, d), const),
            pl.BlockSpec((d, n_main), const),
            pl.BlockSpec((d, LANES), const),
            pl.BlockSpec((1, LANES), const),
        ],
        out_specs=[
            pl.BlockSpec((None, tm, cw), row),
            pl.BlockSpec((None, tm, fw), row),
            pl.BlockSpec((None, tm, fw), row),
            pl.BlockSpec((None, tm, fw), row),
            pl.BlockSpec((None, tm, LANES), row),
            pl.BlockSpec((None, tm, LANES), row),
        ],
        out_shape=[
            jax.ShapeDtypeStruct((b, s, cw), F32),
            jax.ShapeDtypeStruct((b, s, fw), BF16),
            jax.ShapeDtypeStruct((b, s, fw), BF16),
            jax.ShapeDtypeStruct((b, s, fw), BF16),
            jax.ShapeDtypeStruct((b, s, LANES), BF16),
            jax.ShapeDtypeStruct((b, s, LANES), BF16),
        ],
        scratch_shapes=[pltpu.VMEM((1, LANES), F32)],
        compiler_params=_params(("parallel", "arbitrary")),
        name="in_proj",
    )(x, g_mix, w_main, w_f, b_f)


def _conv_kernel(prev_ref, cur_ref, w_ref, cb_ref, lg_ref, lb_ref, o_ref, win_ref, *, rows):
    tc = cur_ref.shape[0]

    @pl.when(pl.program_id(1) == 0)
    def _():
        win_ref[0:HALO, :] = jnp.zeros((HALO, win_ref.shape[1]), F32)

    @pl.when(pl.program_id(1) > 0)
    def _():
        win_ref[0:HALO, :] = prev_ref[tc - HALO:tc, :]

    win_ref[HALO:HALO + tc, :] = cur_ref[...]

    first = HALO - (CONV_K - 1)
    for r in range(tc // rows):
        base = r * rows + first
        acc = win_ref[base:base + rows, :] * w_ref[0:1, :]
        for j in range(1, CONV_K):
            acc = acc + win_ref[base + j:base + j + rows, :] * w_ref[j:j + 1, :]
        y = acc + cb_ref[...]
        mu = jnp.mean(y, axis=-1, keepdims=True)
        yc = y - mu
        yn = yc * lax.rsqrt(jnp.mean(yc * yc, axis=-1, keepdims=True) + EPS)
        yn = yn * lg_ref[...] + lb_ref[...]
        o_ref[r * rows:(r + 1) * rows, :] = (yn * jax.nn.sigmoid(yn)).astype(o_ref.dtype)


def _conv(a, conv_w, conv_b, ln_g, ln_b, *, tc, rows):
    b, s, cw = a.shape
    const = lambda i, j: (0, 0)
    return pl.pallas_call(
        functools.partial(_conv_kernel, rows=rows),
        grid=(b, s // tc),
        in_specs=[
            pl.BlockSpec((None, tc, cw), lambda i, j: (i, jnp.maximum(j - 1, 0), 0)),
            pl.BlockSpec((None, tc, cw), lambda i, j: (i, j, 0)),
            pl.BlockSpec((CONV_K, cw), const),
            pl.BlockSpec((1, cw), const),
            pl.BlockSpec((1, cw), const),
            pl.BlockSpec((1, cw), const),
        ],
        out_specs=pl.BlockSpec((None, tc, cw), lambda i, j: (i, j, 0)),
        out_shape=jax.ShapeDtypeStruct((b, s, cw), BF16),
        scratch_shapes=[pltpu.VMEM((HALO + tc, cw), F32)],
        compiler_params=_params(("parallel", "arbitrary")),
        name="conv",
    )(a, a, conv_w, conv_b, ln_g, ln_b)


def _fox_kernel(q_ref, cq_ref, k_ref, ck_ref, v_ref, o_ref, m_ref, l_ref, acc_ref, *, tb):
    qi = pl.program_id(2)
    pair = pl.program_id(1)
    lane = lax.broadcasted_iota(jnp.int32, (tb, LANES), 1)
    q2 = q_ref[...]
    cq = cq_ref[...]
    zero = jnp.zeros_like(q2)

    lhs = []
    for hh in range(2):
        qmask = (lane >= hh * FOX_HEAD_DIM) & (lane < (hh + 1) * FOX_HEAD_DIM)
        head = pair * 2 + hh
        cmask = (lane >> 3) == head
        lhs.append(jnp.concatenate([jnp.where(qmask, q2, zero), jnp.where(cmask, cq, zero)], axis=-1))
        m_ref[hh] = jnp.full((tb, LANES), -jnp.inf, F32)
        l_ref[hh] = jnp.zeros((tb, LANES), F32)
        acc_ref[hh] = jnp.zeros((tb, LANES), F32)

    def block(j, masked):
        start = pl.multiple_of(j * tb, tb)
        kb = jnp.concatenate([k_ref[pl.ds(start, tb), :], ck_ref[pl.ds(start, tb), :]], axis=-1)
        vb = v_ref[pl.ds(start, tb), :]
        for hh in range(2):
            s = _dot_nt(lhs[hh], kb)
            if masked:
                r = lax.broadcasted_iota(jnp.int32, (tb, tb), 0)
                c = lax.broadcasted_iota(jnp.int32, (tb, tb), 1)
                s = jnp.where(c <= r, s, -jnp.inf)
            m_old = m_ref[hh]
            m_new = jnp.maximum(m_old, jnp.max(s, axis=-1, keepdims=True))
            alpha = jnp.exp(m_old - m_new)
            p = jnp.exp(s - m_new[:, 0:1])
            l_ref[hh] = alpha * l_ref[hh] + jnp.sum(p, axis=-1, keepdims=True)
            acc_ref[hh] = alpha * acc_ref[hh] + _dot(p.astype(BF16), vb)
            m_ref[hh] = m_new

    def body(j, carry):
        block(j, False)
        return carry

    lax.fori_loop(0, qi, body, 0)
    block(qi, True)

    o0 = acc_ref[0] / l_ref[0]
    o1 = acc_ref[1] / l_ref[1]
    o_ref[...] = jnp.where(lane < FOX_HEAD_DIM, o0, o1).astype(o_ref.dtype)


def _fox(q, k, v, cq, ck, *, tb):
    b, s, fw = q.shape
    pairs = fw // LANES
    qspec = pl.BlockSpec((None, tb, LANES), lambda i, p, j: (i, j, p))
    cqspec = pl.BlockSpec((None, tb, LANES), lambda i, p, j: (i, j, 0))
    kspec = pl.BlockSpec((None, s, LANES), lambda i, p, j: (i, 0, p))
    ckspec = pl.BlockSpec((None, s, LANES), lambda i, p, j: (i, 0, 0))
    return pl.pallas_call(
        functools.partial(_fox_kernel, tb=tb),
        grid=(b, pairs, s // tb),
        in_specs=[qspec, cqspec, kspec, ckspec, kspec],
        out_specs=qspec,
        out_shape=jax.ShapeDtypeStruct((b, s, fw), BF16),
        scratch_shapes=[pltpu.VMEM((2, tb, LANES), F32)] * 3,
        compiler_params=_params(("parallel", "parallel", "arbitrary")),
        name="fox",
    )(q, cq, k, ck, v)


def _mix_mem_kernel(x_ref, conv_ref, att_ref, wo_ref, gx_ref, wq_ref, mk_ref, mv_ref, wmo_ref, o_ref):
    cw = conv_ref.shape[-1]
    d = x_ref.shape[-1]
    dh = d // MEM_HEADS
    x1 = x_ref[...] + _dot(conv_ref[...], wo_ref[0:cw, :]) + _dot(att_ref[...], wo_ref[cw:, :])
    hx = _rms(x1, gx_ref[...]).astype(BF16)
    qm = (_dot(hx, wq_ref[...]) * (1.0 / math.sqrt(dh))).astype(BF16)
    outs = []
    for h in range(MEM_HEADS):
        s = _dot_nt(qm[:, h * dh:(h + 1) * dh], mk_ref[:, h * dh:(h + 1) * dh])
        e = jnp.exp(s - jnp.max(s, axis=-1, keepdims=True))
        p = e / jnp.sum(e, axis=-1, keepdims=True)
        outs.append(_dot(p.astype(BF16), mv_ref[:, h * dh:(h + 1) * dh]).astype(BF16))
    o = jnp.concatenate(outs, axis=-1)
    o_ref[...] = x1 + _dot(o, wmo_ref[...])


def _mix_mem(x, conv_out, att_out, w_out, g_x, w_mq, mk, mv, w_mo, *, tm):
    b, s, d = x.shape
    cw = conv_out.shape[-1]
    fw = att_out.shape[-1]
    m = mk.shape[1]
    row = lambda i, j: (i, j, 0)
    const = lambda i, j: (0, 0)
    batch = lambda i, j: (i, 0, 0)
    return pl.pallas_call(
        _mix_mem_kernel,
        grid=(b, s // tm),
        in_specs=[
            pl.BlockSpec((None, tm, d), row),
            pl.BlockSpec((None, tm, cw), row),
            pl.BlockSpec((None, tm, fw), row),
            pl.BlockSpec((cw + fw, d), const),
            pl.BlockSpec((1, d), const),
            pl.BlockSpec((d, d), const),
            pl.BlockSpec((None, m, d), batch),
            pl.BlockSpec((None, m, d), batch),
            pl.BlockSpec((d, d), const),
        ],
        out_specs=pl.BlockSpec((None, tm, d), row),
        out_shape=jax.ShapeDtypeStruct((b, s, d), F32),
        compiler_params=_params(("parallel", "parallel")),
        name="mix_mem",
    )(x, conv_out, att_out, w_out, g_x, w_mq, mk, mv, w_mo)


def _ffn_kernel(x_ref, gf_ref, wg_ref, wu_ref, wd_ref, gl_ref, o_ref):
    x2 = x_ref[...]
    hf = _rms(x2, gf_ref[...]).astype(BF16)
    g = _dot(hf, wg_ref[...])
    u = _dot(hf, wu_ref[...])
    act = (g * jax.nn.sigmoid(g) * u).astype(BF16)
    x3 = x2 + _dot(act, wd_ref[...])
    o_ref[...] = _rms(x3, gl_ref[...])


def _ffn(x2, g_ffn, w_g, w_u, w_down, g_final, *, tm):
    t, d = x2.shape
    dff = w_g.shape[1]
    row = lambda i: (i, 0)
    const = lambda i: (0, 0)
    return pl.pallas_call(
        _ffn_kernel,
        grid=(t // tm,),
        in_specs=[
            pl.BlockSpec((tm, d), row),
            pl.BlockSpec((1, d), const),
            pl.BlockSpec((d, dff), const),
            pl.BlockSpec((d, dff), const),
            pl.BlockSpec((dff, d), const),
            pl.BlockSpec((1, d), const),
        ],
        out_specs=pl.BlockSpec((tm, d), row),
        out_shape=jax.ShapeDtypeStruct((t, d), F32),
        compiler_params=_params(("parallel",)),
        name="ffn",
    )(x2, g_ffn, w_g, w_u, w_down, g_final)


def _tile(n, want):
    t = min(n, want)
    assert n % t == 0, (n, t)
    return t


def kernel(x, mem, g_mix, w_in, b_f, conv_w, conv_b, ln_g, ln_b, w_out, g_x, g_mem, w_mq, w_mkv,
           w_mo, g_ffn, w_gu, w_down, g_final):
    b, s, d = x.shape
    cw = conv_w.shape[-1]
    fw = FOX_HEADS * FOX_HEAD_DIM
    n_main = 2 * cw + 3 * fw
    dff = w_down.shape[1]
    assert w_in.shape[0] == 1, "single-layer block only"
    assert w_in.shape[-1] == n_main + FOX_HEADS and conv_w.shape[1] == CONV_K
    assert fw % LANES == 0 and FOX_HEADS * 8 <= LANES

    mk, mv = _mem_kv(mem, g_mem.reshape(1, d), w_mkv[0].astype(BF16))

    w_main = w_in[0][:, :n_main].astype(BF16)
    w_f = jnp.repeat(w_in[0][:, n_main:], 8, axis=1)
    w_f = jnp.pad(w_f, ((0, 0), (0, LANES - w_f.shape[1]))).astype(BF16)
    bf = jnp.pad(jnp.repeat(b_f[0], 8), (0, LANES - FOX_HEADS * 8)).reshape(1, LANES)

    a, q, k, v, cq, ck = _in_proj(x, g_mix[0].reshape(1, d), w_main, w_f, bf,
                                  tm=_tile(s, 512), cw=cw, fw=fw)
    conv_out = _conv(a, conv_w[0], conv_b[0].reshape(1, cw), ln_g[0].reshape(1, cw),
                     ln_b[0].reshape(1, cw), tc=_tile(s, 256), rows=32)
    att_out = _fox(q, k, v, cq, ck, tb=_tile(s, 256))
    x2 = _mix_mem(x, conv_out, att_out, w_out[0].astype(BF16), g_x[0].reshape(1, d),
                  w_mq[0].astype(BF16), mk, mv, w_mo[0].astype(BF16), tm=_tile(s, 512))
    y = _ffn(x2.reshape(b * s, d), g_ffn[0].reshape(1, d), w_gu[0][:, :dff].astype(BF16),
             w_gu[0][:, dff:].astype(BF16), w_down[0].astype(BF16), g_final.reshape(1, d),
             tm=_tile(b * s, 512))
    return y.reshape(b, s, d)
```

```python
import functools
import math

import jax
import jax.numpy as jnp
from jax import lax
from jax.experimental import pallas as pl
from jax.experimental.pallas import tpu as pltpu

F32 = jnp.float32
BF16 = jnp.bfloat16

EPS = 1e-6
CONV_K = 31
FOX_HEADS = 8
FOX_HEAD_DIM = 64
MEM_HEADS = 4

LANES = 128
CUM_BLOCK = 128
HALO = 32
VMEM_LIMIT = 56 * 1024 * 1024


def _params(semantics):
    return pltpu.CompilerParams(dimension_semantics=semantics, vmem_limit_bytes=VMEM_LIMIT)


def _rms(x, g):
    return x * lax.rsqrt(jnp.mean(x * x, axis=-1, keepdims=True) + EPS) * g


def _split3(x):
    hi = x.astype(BF16)
    r1 = x - hi.astype(F32)
    mid = r1.astype(BF16)
    lo = (r1 - mid.astype(F32)).astype(BF16)
    return hi, mid, lo


def _dot(a, b):
    return jnp.dot(a, b, preferred_element_type=F32)


def _dot_nt(a, b):
    return lax.dot_general(a, b, (((1,), (1,)), ((), ())), preferred_element_type=F32)


def _mem_kv_kernel(mem_ref, g_ref, w_ref, k_ref, v_ref):
    d = mem_ref.shape[-1]
    mn = _rms(mem_ref[...], g_ref[...]).astype(BF16)
    k_ref[...] = _dot(mn, w_ref[:, :d]).astype(BF16)
    v_ref[...] = _dot(mn, w_ref[:, d:]).astype(BF16)


def _mem_kv(mem, g_mem, w_mkv):
    b, m, d = mem.shape
    return pl.pallas_call(
        _mem_kv_kernel,
        grid=(b,),
        in_specs=[
            pl.BlockSpec((None, m, d), lambda i: (i, 0, 0)),
            pl.BlockSpec((1, d), lambda i: (0, 0)),
            pl.BlockSpec((d, 2 * d), lambda i: (0, 0)),
        ],
        out_specs=[
            pl.BlockSpec((None, m, d), lambda i: (i, 0, 0)),
            pl.BlockSpec((None, m, d), lambda i: (i, 0, 0)),
        ],
        out_shape=[jax.ShapeDtypeStruct((b, m, d), BF16)] * 2,
        compiler_params=_params(("parallel",)),
        name="mem_kv",
    )(mem, g_mem, w_mkv)


def _in_proj_kernel(x_ref, g_ref, w_ref, wvt_ref, wf_ref, bf_ref,
                    a_ref, q_ref, k_ref, vt_ref, cq_ref, ck_ref, carry_ref, *, cw, fw):
    @pl.when(pl.program_id(1) == 0)
    def _():
        carry_ref[...] = jnp.zeros_like(carry_ref)

    tm = x_ref.shape[0]
    h = _rms(x_ref[...], g_ref[...]).astype(BF16)

    u = _dot(h, w_ref[:, 0:cw])
    gate = _dot(h, w_ref[:, cw:2 * cw])
    a_ref[...] = u * jax.nn.sigmoid(gate)
    off = 2 * cw
    scale = 1.0 / math.sqrt(FOX_HEAD_DIM)
    q_ref[...] = (_dot(h, w_ref[:, off:off + fw]) * scale).astype(BF16)
    k_ref[...] = _dot(h, w_ref[:, off + fw:off + 2 * fw]).astype(BF16)
    vt_ref[...] = _dot_nt(wvt_ref[...], h).astype(BF16)

    zf = _dot(h, wf_ref[...]) + bf_ref[...]
    logf = jnp.minimum(zf, 0.0) - jnp.log1p(jnp.exp(-jnp.abs(zf)))

    row = lax.broadcasted_iota(jnp.int32, (CUM_BLOCK, CUM_BLOCK), 0)
    col = lax.broadcasted_iota(jnp.int32, (CUM_BLOCK, CUM_BLOCK), 1)
    tri = (col <= row).astype(BF16)

    lane = lax.broadcasted_iota(jnp.int32, (CUM_BLOCK, LANES), 1)
    sub = lane & 7
    valid = lane < FOX_HEADS * 8
    carry = carry_ref[...]
    for r in range(tm // CUM_BLOCK):
        blk = logf[r * CUM_BLOCK:(r + 1) * CUM_BLOCK, :]
        hi, mid, lo = _split3(blk)
        c = (_dot(tri, hi) + _dot(tri, mid)) + _dot(tri, lo) + carry
        carry = c[CUM_BLOCK - 1:CUM_BLOCK, :]
        chi, cmid, clo = (t.astype(F32) for t in _split3(c))
        one = jnp.ones_like(c)
        zero = jnp.zeros_like(c)
        cq = jnp.where(sub == 0, chi, jnp.where(sub == 1, cmid, jnp.where(sub == 2, clo,
             jnp.where(sub < 6, one, zero))))
        ck = jnp.where(sub < 3, one, jnp.where(sub == 3, -chi, jnp.where(sub == 4, -cmid,
             jnp.where(sub == 5, -clo, zero))))
        cq_ref[r * CUM_BLOCK:(r + 1) * CUM_BLOCK, :] = jnp.where(valid, cq, zero).astype(BF16)
        ck_ref[r * CUM_BLOCK:(r + 1) * CUM_BLOCK, :] = jnp.where(valid, ck, zero).astype(BF16)
    carry_ref[...] = carry


def _in_proj(x, g_mix, w_main, w_vt, w_f, b_f, *, tm, cw, fw):
    b, s, d = x.shape
    n_main = w_main.shape[1]
    row = lambda i, j: (i, j, 0)
    const = lambda i, j: (0, 0)
    return pl.pallas_call(
        functools.partial(_in_proj_kernel, cw=cw, fw=fw),
        grid=(b, s // tm),
        in_specs=[
            pl.BlockSpec((None, tm, d), row),
            pl.BlockSpec((1, d), const),
            pl.BlockSpec((d, n_main), const),
            pl.BlockSpec((fw, d), const),
            pl.BlockSpec((d, LANES), const),
            pl.BlockSpec((1, LANES), const),
        ],
        out_specs=[
            pl.BlockSpec((None, tm, cw), row),
            pl.BlockSpec((None, tm, fw), row),
            pl.BlockSpec((None, tm, fw), row),
            pl.BlockSpec((None, fw, tm), lambda i, j: (i, 0, j)),
            pl.BlockSpec((None, tm, LANES), row),
            pl.BlockSpec((None, tm, LANES), row),
        ],
        out_shape=[
            jax.ShapeDtypeStruct((b, s, cw), F32),
            jax.ShapeDtypeStruct((b, s, fw), BF16),
            jax.ShapeDtypeStruct((b, s, fw), BF16),
            jax.ShapeDtypeStruct((b, fw, s), BF16),
            jax.ShapeDtypeStruct((b, s, LANES), BF16),
            jax.ShapeDtypeStruct((b, s, LANES), BF16),
        ],
        scratch_shapes=[pltpu.VMEM((1, LANES), F32)],
        compiler_params=_params(("parallel", "arbitrary")),
        name="in_proj",
    )(x, g_mix, w_main, w_vt, w_f, b_f)


def _conv_kernel(prev_ref, cur_ref, w_ref, cb_ref, lg_ref, lb_ref, o_ref, win_ref, *, rows):
    tc = cur_ref.shape[0]

    @pl.when(pl.program_id(1) == 0)
    def _():
        win_ref[0:HALO, :] = jnp.zeros((HALO, win_ref.shape[1]), F32)

    @pl.when(pl.program_id(1) > 0)
    def _():
        win_ref[0:HALO, :] = prev_ref[tc - HALO:tc, :]

    win_ref[HALO:HALO + tc, :] = cur_ref[...]

    first = HALO - (CONV_K - 1)
    for r in range(tc // rows):
        base = r * rows + first
        acc = win_ref[base:base + rows, :] * w_ref[0:1, :]
        for j in range(1, CONV_K):
            acc = acc + win_ref[base + j:base + j + rows, :] * w_ref[j:j + 1, :]
        y = acc + cb_ref[...]
        mu = jnp.mean(y, axis=-1, keepdims=True)
        yc = y - mu
        yn = yc * lax.rsqrt(jnp.mean(yc * yc, axis=-1, keepdims=True) + EPS)
        yn = yn * lg_ref[...] + lb_ref[...]
        o_ref[r * rows:(r + 1) * rows, :] = (yn * jax.nn.sigmoid(yn)).astype(o_ref.dtype)


def _conv(a, conv_w, conv_b, ln_g, ln_b, *, tc, rows):
    b, s, cw = a.shape
    const = lambda i, j: (0, 0)
    return pl.pallas_call(
        functools.partial(_conv_kernel, rows=rows),
        grid=(b, s // tc),
        in_specs=[
            pl.BlockSpec((None, tc, cw), lambda i, j: (i, jnp.maximum(j - 1, 0), 0)),
            pl.BlockSpec((None, tc, cw), lambda i, j: (i, j, 0)),
            pl.BlockSpec((CONV_K, cw), const),
            pl.BlockSpec((1, cw), const),
            pl.BlockSpec((1, cw), const),
            pl.BlockSpec((1, cw), const),
        ],
        out_specs=pl.BlockSpec((None, tc, cw), lambda i, j: (i, j, 0)),
        out_shape=jax.ShapeDtypeStruct((b, s, cw), BF16),
        scratch_shapes=[pltpu.VMEM((HALO + tc, cw), F32)],
        compiler_params=_params(("parallel", "arbitrary")),
        name="conv",
    )(a, a, conv_w, conv_b, ln_g, ln_b)


def _fox_kernel(q_ref, cq_ref, k_ref, ck_ref, vt_ref, o_ref, s_ref, p_ref, *, tb):
    pair = pl.program_id(1)
    s_len = q_ref.shape[0]
    lane = lax.broadcasted_iota(jnp.int32, (tb, LANES), 1)
    key = lax.broadcasted_iota(jnp.int32, (tb, tb), 0)
    qry = lax.broadcasted_iota(jnp.int32, (tb, tb), 1)
    ones = jnp.ones((16, s_len), BF16)

    for qi in range(s_len // tb):
        lo, hi = qi * tb, (qi + 1) * tb
        q2 = q_ref[lo:hi, :]
        cq = cq_ref[lo:hi, :]
        zero = jnp.zeros_like(q2)
        kcat = jnp.concatenate([k_ref[0:hi, :], ck_ref[0:hi, :]], axis=-1)
        outs = []
        for hh in range(2):
            qmask = (lane >= hh * FOX_HEAD_DIM) & (lane < (hh + 1) * FOX_HEAD_DIM)
            cmask = (lane >> 3) == pair * 2 + hh
            qcat = jnp.concatenate([jnp.where(qmask, q2, zero), jnp.where(cmask, cq, zero)], axis=-1)
            s_ref[0:hi, :] = _dot_nt(kcat, qcat)
            s_ref[lo:hi, :] = jnp.where(key <= qry, s_ref[lo:hi, :], -jnp.inf)
            m = jnp.max(s_ref[0:hi, :], axis=0, keepdims=True)
            p_ref[0:hi, :] = jnp.exp(s_ref[0:hi, :] - m).astype(BF16)
            vaug = jnp.concatenate([vt_ref[hh * FOX_HEAD_DIM:(hh + 1) * FOX_HEAD_DIM, 0:hi],
                                    ones[:, 0:hi]], axis=0)
            ot = _dot(vaug, p_ref[0:hi, :])
            outs.append(ot[0:FOX_HEAD_DIM, :] / ot[FOX_HEAD_DIM:FOX_HEAD_DIM + 1, :])
        o_ref[lo:hi, :] = jnp.concatenate(outs, axis=0).T.astype(o_ref.dtype)


def _fox(q, k, vt, cq, ck, *, tb):
    b, s, fw = q.shape
    pairs = fw // LANES
    qspec = pl.BlockSpec((None, s, LANES), lambda i, p: (i, 0, p))
    cspec = pl.BlockSpec((None, s, LANES), lambda i, p: (i, 0, 0))
    vspec = pl.BlockSpec((None, LANES, s), lambda i, p: (i, p, 0))
    return pl.pallas_call(
        functools.partial(_fox_kernel, tb=tb),
        grid=(b, pairs),
        in_specs=[qspec, cspec, qspec, cspec, vspec],
        out_specs=qspec,
        out_shape=jax.ShapeDtypeStruct((b, s, fw), BF16),
        scratch_shapes=[pltpu.VMEM((s, tb), F32), pltpu.VMEM((s, tb), BF16)],
        compiler_params=_params(("parallel", "parallel")),
        name="fox",
    )(q, cq, k, ck, vt)


def _mix_mem_kernel(x_ref, conv_ref, att_ref, wo_ref, gx_ref, wq_ref, mk_ref, mv_ref, wmo_ref, o_ref):
    cw = conv_ref.shape[-1]
    d = x_ref.shape[-1]
    dh = d // MEM_HEADS
    x1 = x_ref[...] + _dot(conv_ref[...], wo_ref[0:cw, :]) + _dot(att_ref[...], wo_ref[cw:, :])
    hx = _rms(x1, gx_ref[...]).astype(BF16)
    qm = (_dot(hx, wq_ref[...]) * (1.0 / math.sqrt(dh))).astype(BF16)
    outs = []
    for h in range(MEM_HEADS):
        s = _dot_nt(qm[:, h * dh:(h + 1) * dh], mk_ref[:, h * dh:(h + 1) * dh])
        e = jnp.exp(s - jnp.max(s, axis=-1, keepdims=True))
        p = e / jnp.sum(e, axis=-1, keepdims=True)
        outs.append(_dot(p.astype(BF16), mv_ref[:, h * dh:(h + 1) * dh]).astype(BF16))
    o = jnp.concatenate(outs, axis=-1)
    o_ref[...] = x1 + _dot(o, wmo_ref[...])


def _mix_mem(x, conv_out, att_out, w_out, g_x, w_mq, mk, mv, w_mo, *, tm):
    b, s, d = x.shape
    cw = conv_out.shape[-1]
    fw = att_out.shape[-1]
    m = mk.shape[1]
    row = lambda i, j: (i, j, 0)
    const = lambda i, j: (0, 0)
    batch = lambda i, j: (i, 0, 0)
    return pl.pallas_call(
        _mix_mem_kernel,
        grid=(b, s // tm),
        in_specs=[
            pl.BlockSpec((None, tm, d), row),
            pl.BlockSpec((None, tm, cw), row),
            pl.BlockSpec((None, tm, fw), row),
            pl.BlockSpec((cw + fw, d), const),
            pl.BlockSpec((1, d), const),
            pl.BlockSpec((d, d), const),
            pl.BlockSpec((None, m, d), batch),
            pl.BlockSpec((None, m, d), batch),
            pl.BlockSpec((d, d), const),
        ],
        out_specs=pl.BlockSpec((None, tm, d), row),
        out_shape=jax.ShapeDtypeStruct((b, s, d), F32),
        compiler_params=_params(("parallel", "parallel")),
        name="mix_mem",
    )(x, conv_out, att_out, w_out, g_x, w_mq, mk, mv, w_mo)


def _ffn_kernel(x_ref, gf_ref, wg_ref, wu_ref, wd_ref, gl_ref, o_ref):
    x2 = x_ref[...]
    hf = _rms(x2, gf_ref[...]).astype(BF16)
    g = _dot(hf, wg_ref[...])
    u = _dot(hf, wu_ref[...])
    act = (g * jax.nn.sigmoid(g) * u).astype(BF16)
    x3 = x2 + _dot(act, wd_ref[...])
    o_ref[...] = _rms(x3, gl_ref[...])


def _ffn(x2, g_ffn, w_g, w_u, w_down, g_final, *, tm):
    t, d = x2.shape
    dff = w_g.shape[1]
    row = lambda i: (i, 0)
    const = lambda i: (0, 0)
    return pl.pallas_call(
        _ffn_kernel,
        grid=(t // tm,),
        in_specs=[
            pl.BlockSpec((tm, d), row),
            pl.BlockSpec((1, d), const),
            pl.BlockSpec((d, dff), const),
            pl.BlockSpec((d, dff), const),
            pl.BlockSpec((dff, d), const),
            pl.BlockSpec((1, d), const),
        ],
        out_specs=pl.BlockSpec((tm, d), row),
        out_shape=jax.ShapeDtypeStruct((t, d), F32),
        compiler_params=_params(("parallel",)),
        name="ffn",
    )(x2, g_ffn, w_g, w_u, w_down, g_final)


def _tile(n, want):
    t = min(n, want)
    assert n % t == 0, (n, t)
    return t


def kernel(x, mem, g_mix, w_in, b_f, conv_w, conv_b, ln_g, ln_b, w_out, g_x, g_mem, w_mq, w_mkv,
           w_mo, g_ffn, w_gu, w_down, g_final):
    b, s, d = x.shape
    cw = conv_w.shape[-1]
    fw = FOX_HEADS * FOX_HEAD_DIM
    n_main = 2 * cw + 2 * fw
    dff = w_down.shape[1]
    assert w_in.shape[0] == 1, "single-layer block only"
    assert w_in.shape[-1] == n_main + fw + FOX_HEADS and conv_w.shape[1] == CONV_K
    assert fw % LANES == 0 and FOX_HEADS * 8 <= LANES

    mk, mv = _mem_kv(mem, g_mem.reshape(1, d), w_mkv[0].astype(BF16))

    w_main = w_in[0][:, :n_main].astype(BF16)
    w_vt = w_in[0][:, n_main:n_main + fw].T.astype(BF16)
    w_f = jnp.repeat(w_in[0][:, n_main + fw:], 8, axis=1)
    w_f = jnp.pad(w_f, ((0, 0), (0, LANES - w_f.shape[1]))).astype(BF16)
    bf = jnp.pad(jnp.repeat(b_f[0], 8), (0, LANES - FOX_HEADS * 8)).reshape(1, LANES)

    a, q, k, vt, cq, ck = _in_proj(x, g_mix[0].reshape(1, d), w_main, w_vt, w_f, bf,
                                   tm=_tile(s, 512), cw=cw, fw=fw)
    conv_out = _conv(a, conv_w[0], conv_b[0].reshape(1, cw), ln_g[0].reshape(1, cw),
                     ln_b[0].reshape(1, cw), tc=_tile(s, 256), rows=32)
    att_out = _fox(q, k, vt, cq, ck, tb=_tile(s, 256))
    x2 = _mix_mem(x, conv_out, att_out, w_out[0].astype(BF16), g_x[0].reshape(1, d),
                  w_mq[0].astype(BF16), mk, mv, w_mo[0].astype(BF16), tm=_tile(s, 512))
    y = _ffn(x2.reshape(b * s, d), g_ffn[0].reshape(1, d), w_gu[0][:, :dff].astype(BF16),
             w_gu[0][:, dff:].astype(BF16), w_down[0].astype(BF16), g_final.reshape(1, d),
             tm=_tile(b * s, 512))
    return y.reshape(b, s, d)
```

```python
import functools
import math

import jax
import jax.numpy as jnp
from jax import lax
from jax.experimental import pallas as pl
from jax.experimental.pallas import tpu as pltpu

F32 = jnp.float32
BF16 = jnp.bfloat16

EPS = 1e-6
CONV_K = 31
FOX_HEADS = 8
FOX_HEAD_DIM = 64
MEM_HEADS = 4

LANES = 128
CUM_BLOCK = 128
HALO = 32
VMEM_LIMIT = 56 * 1024 * 1024


def _params(semantics):
    return pltpu.CompilerParams(dimension_semantics=semantics, vmem_limit_bytes=VMEM_LIMIT)


def _rms(x, g):
    return x * lax.rsqrt(jnp.mean(x * x, axis=-1, keepdims=True) + EPS) * g


def _split3(x):
    hi = x.astype(BF16)
    r1 = x - hi.astype(F32)
    mid = r1.astype(BF16)
    lo = (r1 - mid.astype(F32)).astype(BF16)
    return hi, mid, lo


def _dot(a, b):
    return jnp.dot(a, b, preferred_element_type=F32)


def _dot_nt(a, b):
    return lax.dot_general(a, b, (((1,), (1,)), ((), ())), preferred_element_type=F32)


def _mem_kv_kernel(mem_ref, g_ref, w_ref, k_ref, v_ref):
    d = mem_ref.shape[-1]
    mn = _rms(mem_ref[...], g_ref[...]).astype(BF16)
    k_ref[...] = _dot(mn, w_ref[:, :d]).astype(BF16)
    v_ref[...] = _dot(mn, w_ref[:, d:]).astype(BF16)


def _mem_kv(mem, g_mem, w_mkv):
    b, m, d = mem.shape
    return pl.pallas_call(
        _mem_kv_kernel,
        grid=(b,),
        in_specs=[
            pl.BlockSpec((None, m, d), lambda i: (i, 0, 0)),
            pl.BlockSpec((1, d), lambda i: (0, 0)),
            pl.BlockSpec((d, 2 * d), lambda i: (0, 0)),
        ],
        out_specs=[
            pl.BlockSpec((None, m, d), lambda i: (i, 0, 0)),
            pl.BlockSpec((None, m, d), lambda i: (i, 0, 0)),
        ],
        out_shape=[jax.ShapeDtypeStruct((b, m, d), BF16)] * 2,
        compiler_params=_params(("parallel",)),
        name="mem_kv",
    )(mem, g_mem, w_mkv)


def _in_proj_kernel(x_ref, g_ref, w_ref, wvt_ref, wf_ref, bf_ref,
                    a_ref, q_ref, k_ref, vt_ref, cq_ref, ck_ref, carry_ref, *, cw, fw):
    @pl.when(pl.program_id(1) == 0)
    def _():
        carry_ref[...] = jnp.zeros_like(carry_ref)

    tm = x_ref.shape[0]
    h = _rms(x_ref[...], g_ref[...]).astype(BF16)

    u = _dot(h, w_ref[:, 0:cw])
    gate = _dot(h, w_ref[:, cw:2 * cw])
    a_ref[...] = u * jax.nn.sigmoid(gate)
    off = 2 * cw
    scale = 1.0 / math.sqrt(FOX_HEAD_DIM)
    q_ref[...] = (_dot(h, w_ref[:, off:off + fw]) * scale).astype(BF16)
    k_ref[...] = _dot(h, w_ref[:, off + fw:off + 2 * fw]).astype(BF16)
    vt_ref[...] = _dot_nt(wvt_ref[...], h).astype(BF16)

    zf = _dot(h, wf_ref[...]) + bf_ref[...]
    logf = jnp.minimum(zf, 0.0) - jnp.log1p(jnp.exp(-jnp.abs(zf)))

    row = lax.broadcasted_iota(jnp.int32, (CUM_BLOCK, CUM_BLOCK), 0)
    col = lax.broadcasted_iota(jnp.int32, (CUM_BLOCK, CUM_BLOCK), 1)
    tri = (col <= row).astype(BF16)

    lane = lax.broadcasted_iota(jnp.int32, (CUM_BLOCK, LANES), 1)
    sub = lane & 7
    valid = lane < FOX_HEADS * 8
    carry = carry_ref[...]
    for r in range(tm // CUM_BLOCK):
        blk = logf[r * CUM_BLOCK:(r + 1) * CUM_BLOCK, :]
        hi, mid, lo = _split3(blk)
        c = (_dot(tri, hi) + _dot(tri, mid)) + _dot(tri, lo) + carry
        carry = c[CUM_BLOCK - 1:CUM_BLOCK, :]
        chi, cmid, clo = (t.astype(F32) for t in _split3(c))
        one = jnp.ones_like(c)
        zero = jnp.zeros_like(c)
        cq = jnp.where(sub == 0, chi, jnp.where(sub == 1, cmid, jnp.where(sub == 2, clo,
             jnp.where(sub < 6, one, zero))))
        ck = jnp.where(sub < 3, one, jnp.where(sub == 3, -chi, jnp.where(sub == 4, -cmid,
             jnp.where(sub == 5, -clo, zero))))
        cq_ref[r * CUM_BLOCK:(r + 1) * CUM_BLOCK, :] = jnp.where(valid, cq, zero).astype(BF16)
        ck_ref[r * CUM_BLOCK:(r + 1) * CUM_BLOCK, :] = jnp.where(valid, ck, zero).astype(BF16)
    carry_ref[...] = carry


def _in_proj(x, g_mix, w_main, w_vt, w_f, b_f, *, tm, cw, fw):
    b, s, d = x.shape
    n_main = w_main.shape[1]
    row = lambda i, j: (i, j, 0)
    const = lambda i, j: (0, 0)
    return pl.pallas_call(
        functools.partial(_in_proj_kernel, cw=cw, fw=fw),
        grid=(b, s // tm),
        in_specs=[
            pl.BlockSpec((None, tm, d), row),
            pl.BlockSpec((1, d), const),
            pl.BlockSpec((d, n_main), const),
            pl.BlockSpec((fw, d), const),
            pl.BlockSpec((d, LANES), const),
            pl.BlockSpec((1, LANES), const),
        ],
        out_specs=[
            pl.BlockSpec((None, tm, cw), row),
            pl.BlockSpec((None, tm, fw), row),
            pl.BlockSpec((None, tm, fw), row),
            pl.BlockSpec((None, fw, tm), lambda i, j: (i, 0, j)),
            pl.BlockSpec((None, tm, LANES), row),
            pl.BlockSpec((None, tm, LANES), row),
        ],
        out_shape=[
            jax.ShapeDtypeStruct((b, s, cw), F32),
            jax.ShapeDtypeStruct((b, s, fw), BF16),
            jax.ShapeDtypeStruct((b, s, fw), BF16),
            jax.ShapeDtypeStruct((b, fw, s), BF16),
            jax.ShapeDtypeStruct((b, s, LANES), BF16),
            jax.ShapeDtypeStruct((b, s, LANES), BF16),
        ],
        scratch_shapes=[pltpu.VMEM((1, LANES), F32)],
        compiler_params=_params(("parallel", "arbitrary")),
        name="in_proj",
    )(x, g_mix, w_main, w_vt, w_f, b_f)


def _conv_kernel(prev_ref, cur_ref, w_ref, cb_ref, lg_ref, lb_ref, o_ref, win_ref, sh_ref, *, rows):
    tc = cur_ref.shape[0]

    @pl.when(pl.program_id(1) == 0)
    def _():
        win_ref[0:HALO, :] = jnp.zeros((HALO, win_ref.shape[1]), F32)

    @pl.when(pl.program_id(1) > 0)
    def _():
        win_ref[0:HALO, :] = prev_ref[tc - HALO:tc, :]

    win_ref[HALO:HALO + tc, :] = cur_ref[...]

    for r in range(1, 8):
        sh_ref[r, 8:HALO + tc, :] = win_ref[8 - r:HALO + tc - r, :]

    def shifted(r, start):
        src = win_ref[start:start + rows, :] if r == 0 else sh_ref[r, start:start + rows, :]
        return src.reshape(rows // 8, 8, src.shape[-1])

    for c in range(tc // rows):
        base = HALO + c * rows
        acc = None
        for lag in range(CONV_K):
            m, r = divmod(lag, 8)
            term = shifted(r, base - 8 * m) * w_ref[CONV_K - 1 - lag]
            acc = term if acc is None else acc + term
        y = acc.reshape(rows, acc.shape[-1]) + cb_ref[...]
        mu = jnp.mean(y, axis=-1, keepdims=True)
        yc = y - mu
        yn = yc * lax.rsqrt(jnp.mean(yc * yc, axis=-1, keepdims=True) + EPS)
        yn = yn * lg_ref[...] + lb_ref[...]
        o_ref[c * rows:(c + 1) * rows, :] = (yn * jax.nn.sigmoid(yn)).astype(o_ref.dtype)


def _conv(a, conv_w, conv_b, ln_g, ln_b, *, tc, rows):
    b, s, cw = a.shape
    const = lambda i, j: (0, 0)
    return pl.pallas_call(
        functools.partial(_conv_kernel, rows=rows),
        grid=(b, s // tc),
        in_specs=[
            pl.BlockSpec((None, tc, cw), lambda i, j: (i, jnp.maximum(j - 1, 0), 0)),
            pl.BlockSpec((None, tc, cw), lambda i, j: (i, j, 0)),
            pl.BlockSpec((CONV_K, 8, cw), lambda i, j: (0, 0, 0)),
            pl.BlockSpec((1, cw), const),
            pl.BlockSpec((1, cw), const),
            pl.BlockSpec((1, cw), const),
        ],
        out_specs=pl.BlockSpec((None, tc, cw), lambda i, j: (i, j, 0)),
        out_shape=jax.ShapeDtypeStruct((b, s, cw), BF16),
        scratch_shapes=[pltpu.VMEM((HALO + tc, cw), F32), pltpu.VMEM((8, HALO + tc, cw), F32)],
        compiler_params=_params(("parallel", "arbitrary")),
        name="conv",
    )(a, a, conv_w, conv_b, ln_g, ln_b)


def _fox_kernel(q_ref, cq_ref, k_ref, ck_ref, vt_ref, o_ref, s_ref, p_ref, *, tb):
    pair = pl.program_id(1)
    s_len = q_ref.shape[0]
    lane = lax.broadcasted_iota(jnp.int32, (tb, LANES), 1)
    key = lax.broadcasted_iota(jnp.int32, (tb, tb), 0)
    qry = lax.broadcasted_iota(jnp.int32, (tb, tb), 1)
    ones = jnp.ones((16, s_len), BF16)

    def logits(qi):
        lo, hi = qi * tb, (qi + 1) * tb
        q2 = q_ref[lo:hi, :]
        cq = cq_ref[lo:hi, :]
        zero = jnp.zeros_like(q2)
        kcat = jnp.concatenate([k_ref[0:hi, :], ck_ref[0:hi, :]], axis=-1)
        for hh in range(2):
            buf = 2 * (qi % 2) + hh
            qmask = (lane >= hh * FOX_HEAD_DIM) & (lane < (hh + 1) * FOX_HEAD_DIM)
            cmask = (lane >> 3) == pair * 2 + hh
            qcat = jnp.concatenate([jnp.where(qmask, q2, zero), jnp.where(cmask, cq, zero)], axis=-1)
            s_ref[buf, 0:hi, :] = _dot_nt(kcat, qcat)
            s_ref[buf, lo:hi, :] = jnp.where(key <= qry, s_ref[buf, lo:hi, :], -jnp.inf)

    def softmax_pv(qi):
        lo, hi = qi * tb, (qi + 1) * tb
        for hh in range(2):
            buf = 2 * (qi % 2) + hh
            m = jnp.max(s_ref[buf, 0:hi, :], axis=0, keepdims=True)
            p_ref[buf, 0:hi, :] = jnp.exp(s_ref[buf, 0:hi, :] - m).astype(BF16)
        outs = []
        for hh in range(2):
            buf = 2 * (qi % 2) + hh
            vaug = jnp.concatenate([vt_ref[hh * FOX_HEAD_DIM:(hh + 1) * FOX_HEAD_DIM, 0:hi],
                                    ones[:, 0:hi]], axis=0)
            ot = _dot(vaug, p_ref[buf, 0:hi, :])
            outs.append(ot[0:FOX_HEAD_DIM, :] / ot[FOX_HEAD_DIM:FOX_HEAD_DIM + 1, :])
        o_ref[lo:hi, :] = jnp.concatenate(outs, axis=0).T.astype(o_ref.dtype)

    nq = s_len // tb
    logits(0)
    for qi in range(nq):
        if qi + 1 < nq:
            logits(qi + 1)
        softmax_pv(qi)


def _fox(q, k, vt, cq, ck, *, tb):
    b, s, fw = q.shape
    pairs = fw // LANES
    qspec = pl.BlockSpec((None, s, LANES), lambda i, p: (i, 0, p))
    cspec = pl.BlockSpec((None, s, LANES), lambda i, p: (i, 0, 0))
    vspec = pl.BlockSpec((None, LANES, s), lambda i, p: (i, p, 0))
    return pl.pallas_call(
        functools.partial(_fox_kernel, tb=tb),
        grid=(b, pairs),
        in_specs=[qspec, cspec, qspec, cspec, vspec],
        out_specs=qspec,
        out_shape=jax.ShapeDtypeStruct((b, s, fw), BF16),
        scratch_shapes=[pltpu.VMEM((4, s, tb), F32), pltpu.VMEM((4, s, tb), BF16)],
        compiler_params=_params(("parallel", "parallel")),
        name="fox",
    )(q, cq, k, ck, vt)


def _mix_mem_kernel(x_ref, conv_ref, att_ref, wo_ref, gx_ref, wq_ref, mk_ref, mv_ref, wmo_ref, o_ref):
    cw = conv_ref.shape[-1]
    d = x_ref.shape[-1]
    dh = d // MEM_HEADS
    x1 = x_ref[...] + _dot(conv_ref[...], wo_ref[0:cw, :]) + _dot(att_ref[...], wo_ref[cw:, :])
    hx = _rms(x1, gx_ref[...]).astype(BF16)
    qm = (_dot(hx, wq_ref[...]) * (1.0 / math.sqrt(dh))).astype(BF16)
    outs = []
    for h in range(MEM_HEADS):
        s = _dot_nt(qm[:, h * dh:(h + 1) * dh], mk_ref[:, h * dh:(h + 1) * dh])
        e = jnp.exp(s - jnp.max(s, axis=-1, keepdims=True))
        p = e / jnp.sum(e, axis=-1, keepdims=True)
        outs.append(_dot(p.astype(BF16), mv_ref[:, h * dh:(h + 1) * dh]).astype(BF16))
    o = jnp.concatenate(outs, axis=-1)
    o_ref[...] = x1 + _dot(o, wmo_ref[...])


def _mix_mem(x, conv_out, att_out, w_out, g_x, w_mq, mk, mv, w_mo, *, tm):
    b, s, d = x.shape
    cw = conv_out.shape[-1]
    fw = att_out.shape[-1]
    m = mk.shape[1]
    row = lambda i, j: (i, j, 0)
    const = lambda i, j: (0, 0)
    batch = lambda i, j: (i, 0, 0)
    return pl.pallas_call(
        _mix_mem_kernel,
        grid=(b, s // tm),
        in_specs=[
            pl.BlockSpec((None, tm, d), row),
            pl.BlockSpec((None, tm, cw), row),
            pl.BlockSpec((None, tm, fw), row),
            pl.BlockSpec((cw + fw, d), const),
            pl.BlockSpec((1, d), const),
            pl.BlockSpec((d, d), const),
            pl.BlockSpec((None, m, d), batch),
            pl.BlockSpec((None, m, d), batch),
            pl.BlockSpec((d, d), const),
        ],
        out_specs=pl.BlockSpec((None, tm, d), row),
        out_shape=jax.ShapeDtypeStruct((b, s, d), F32),
        compiler_params=_params(("parallel", "parallel")),
        name="mix_mem",
    )(x, conv_out, att_out, w_out, g_x, w_mq, mk, mv, w_mo)


def _ffn_kernel(x_ref, gf_ref, wg_ref, wu_ref, wd_ref, gl_ref, o_ref):
    x2 = x_ref[...]
    hf = _rms(x2, gf_ref[...]).astype(BF16)
    g = _dot(hf, wg_ref[...])
    u = _dot(hf, wu_ref[...])
    act = (g * jax.nn.sigmoid(g) * u).astype(BF16)
    x3 = x2 + _dot(act, wd_ref[...])
    o_ref[...] = _rms(x3, gl_ref[...])


def _ffn(x2, g_ffn, w_g, w_u, w_down, g_final, *, tm):
    t, d = x2.shape
    dff = w_g.shape[1]
    row = lambda i: (i, 0)
    const = lambda i: (0, 0)
    return pl.pallas_call(
        _ffn_kernel,
        grid=(t // tm,),
        in_specs=[
            pl.BlockSpec((tm, d), row),
            pl.BlockSpec((1, d), const),
            pl.BlockSpec((d, dff), const),
            pl.BlockSpec((d, dff), const),
            pl.BlockSpec((dff, d), const),
            pl.BlockSpec((1, d), const),
        ],
        out_specs=pl.BlockSpec((tm, d), row),
        out_shape=jax.ShapeDtypeStruct((t, d), F32),
        compiler_params=_params(("parallel",)),
        name="ffn",
    )(x2, g_ffn, w_g, w_u, w_down, g_final)


def _tile(n, want):
    t = min(n, want)
    assert n % t == 0, (n, t)
    return t


def kernel(x, mem, g_mix, w_in, b_f, conv_w, conv_b, ln_g, ln_b, w_out, g_x, g_mem, w_mq, w_mkv,
           w_mo, g_ffn, w_gu, w_down, g_final):
    b, s, d = x.shape
    cw = conv_w.shape[-1]
    fw = FOX_HEADS * FOX_HEAD_DIM
    n_main = 2 * cw + 2 * fw
    dff = w_down.shape[1]
    assert w_in.shape[0] == 1, "single-layer block only"
    assert w_in.shape[-1] == n_main + fw + FOX_HEADS and conv_w.shape[1] == CONV_K
    assert fw % LANES == 0 and FOX_HEADS * 8 <= LANES

    mk, mv = _mem_kv(mem, g_mem.reshape(1, d), w_mkv[0].astype(BF16))

    w_main = w_in[0][:, :n_main].astype(BF16)
    w_vt = w_in[0][:, n_main:n_main + fw].T.astype(BF16)
    w_f = jnp.repeat(w_in[0][:, n_main + fw:], 8, axis=1)
    w_f = jnp.pad(w_f, ((0, 0), (0, LANES - w_f.shape[1]))).astype(BF16)
    bf = jnp.pad(jnp.repeat(b_f[0], 8), (0, LANES - FOX_HEADS * 8)).reshape(1, LANES)

    a, q, k, vt, cq, ck = _in_proj(x, g_mix[0].reshape(1, d), w_main, w_vt, w_f, bf,
                                   tm=_tile(s, 512), cw=cw, fw=fw)
    conv_taps = jnp.broadcast_to(conv_w[0][:, None, :], (CONV_K, 8, cw))
    conv_out = _conv(a, conv_taps, conv_b[0].reshape(1, cw), ln_g[0].reshape(1, cw),
                     ln_b[0].reshape(1, cw), tc=_tile(s, 256), rows=32)
    att_out = _fox(q, k, vt, cq, ck, tb=_tile(s, 256))
    x2 = _mix_mem(x, conv_out, att_out, w_out[0].astype(BF16), g_x[0].reshape(1, d),
                  w_mq[0].astype(BF16), mk, mv, w_mo[0].astype(BF16), tm=_tile(s, 512))
    y = _ffn(x2.reshape(b * s, d), g_ffn[0].reshape(1, d), w_gu[0][:, :dff].astype(BF16),
             w_gu[0][:, dff:].astype(BF16), w_down[0].astype(BF16), g_final.reshape(1, d),
             tm=_tile(b * s, 512))
    return y.reshape(b, s, d)
```

```python
import functools
import math

import jax
import jax.numpy as jnp
from jax import lax
from jax.experimental import pallas as pl
from jax.experimental.pallas import tpu as pltpu

F32 = jnp.float32
BF16 = jnp.bfloat16

EPS = 1e-6
LOG2E = math.log2(math.e)
CONV_K = 31
FOX_HEADS = 8
FOX_HEAD_DIM = 64
MEM_HEADS = 4

LANES = 128
CUM_BLOCK = 128
HALO = 32
VMEM_LIMIT = 56 * 1024 * 1024


def _params(semantics):
    return pltpu.CompilerParams(dimension_semantics=semantics, vmem_limit_bytes=VMEM_LIMIT)


def _rms(x, g):
    return x * lax.rsqrt(jnp.mean(x * x, axis=-1, keepdims=True) + EPS) * g


def _split3(x):
    hi = x.astype(BF16)
    r1 = x - hi.astype(F32)
    mid = r1.astype(BF16)
    lo = (r1 - mid.astype(F32)).astype(BF16)
    return hi, mid, lo


def _dot(a, b):
    return jnp.dot(a, b, preferred_element_type=F32)


def _dot_nt(a, b):
    return lax.dot_general(a, b, (((1,), (1,)), ((), ())), preferred_element_type=F32)


def _mem_kv_kernel(mem_ref, g_ref, w_ref, k_ref, v_ref):
    d = mem_ref.shape[-1]
    mn = _rms(mem_ref[...], g_ref[...]).astype(BF16)
    k_ref[...] = _dot(mn, w_ref[:, :d]).astype(BF16)
    v_ref[...] = _dot(mn, w_ref[:, d:]).astype(BF16)


def _mem_kv(mem, g_mem, w_mkv):
    b, m, d = mem.shape
    return pl.pallas_call(
        _mem_kv_kernel,
        grid=(b,),
        in_specs=[
            pl.BlockSpec((None, m, d), lambda i: (i, 0, 0)),
            pl.BlockSpec((1, d), lambda i: (0, 0)),
            pl.BlockSpec((d, 2 * d), lambda i: (0, 0)),
        ],
        out_specs=[
            pl.BlockSpec((None, m, d), lambda i: (i, 0, 0)),
            pl.BlockSpec((None, m, d), lambda i: (i, 0, 0)),
        ],
        out_shape=[jax.ShapeDtypeStruct((b, m, d), BF16)] * 2,
        compiler_params=_params(("parallel",)),
        name="mem_kv",
    )(mem, g_mem, w_mkv)


def _in_proj_kernel(x_ref, g_ref, w_ref, wvt_ref, wf_ref, bf_ref, taps_ref, cb_ref, lg_ref, lb_ref,
                    conv_ref, q_ref, k_ref, vt_ref, cq_ref, ck_ref,
                    carry_ref, win_ref, sh_ref, *, cw, fw, conv_rows):
    tm = x_ref.shape[0]

    @pl.when(pl.program_id(1) == 0)
    def _():
        carry_ref[...] = jnp.zeros_like(carry_ref)
        win_ref[0:HALO, :] = jnp.zeros((HALO, win_ref.shape[1]), F32)

    @pl.when(pl.program_id(1) > 0)
    def _():
        win_ref[0:HALO, :] = win_ref[tm:tm + HALO, :]

    h = _rms(x_ref[...], g_ref[...]).astype(BF16)

    u = _dot(h, w_ref[:, 0:cw])
    gate = _dot(h, w_ref[:, cw:2 * cw])
    win_ref[HALO:HALO + tm, :] = u * jax.nn.sigmoid(gate)

    off = 2 * cw
    scale = LOG2E / math.sqrt(FOX_HEAD_DIM)
    half = fw // 2

    def q_piece(n):
        def run():
            cols = slice(n * half, (n + 1) * half)
            q_ref[:, cols] = (_dot(h, w_ref[:, off + n * half:off + (n + 1) * half]) * scale).astype(BF16)
        return run

    def k_piece(n):
        def run():
            cols = slice(n * half, (n + 1) * half)
            k_ref[:, cols] = _dot(h, w_ref[:, off + fw + n * half:off + fw + (n + 1) * half]).astype(BF16)
        return run

    def vt_piece(n):
        def run():
            rows = slice(n * half, (n + 1) * half)
            vt_ref[rows, :] = _dot_nt(wvt_ref[rows, :], h).astype(BF16)
        return run

    mxu_pieces = [q_piece(0), q_piece(1), k_piece(0), k_piece(1), vt_piece(0), vt_piece(1)]
    conv_pieces = _conv_pieces(win_ref, sh_ref, taps_ref, cb_ref, lg_ref, lb_ref, conv_ref, rows=conv_rows)
    every = -(-len(conv_pieces) // len(mxu_pieces))
    for n, piece in enumerate(conv_pieces):
        piece()
        if n % every == every - 1 and mxu_pieces:
            mxu_pieces.pop(0)()
    for piece in mxu_pieces:
        piece()

    zf = _dot(h, wf_ref[...]) + bf_ref[...]
    logf = jnp.minimum(zf, 0.0) - jnp.log1p(jnp.exp(-jnp.abs(zf)))
    logf = logf * LOG2E

    row = lax.broadcasted_iota(jnp.int32, (CUM_BLOCK, CUM_BLOCK), 0)
    col = lax.broadcasted_iota(jnp.int32, (CUM_BLOCK, CUM_BLOCK), 1)
    tri = (col <= row).astype(BF16)

    lane = lax.broadcasted_iota(jnp.int32, (CUM_BLOCK, LANES), 1)
    sub = lane & 7
    valid = lane < FOX_HEADS * 8
    carry = carry_ref[...]
    for r in range(tm // CUM_BLOCK):
        blk = logf[r * CUM_BLOCK:(r + 1) * CUM_BLOCK, :]
        hi, mid, lo = _split3(blk)
        c = (_dot(tri, hi) + _dot(tri, mid)) + _dot(tri, lo) + carry
        carry = c[CUM_BLOCK - 1:CUM_BLOCK, :]
        chi, cmid, clo = (t.astype(F32) for t in _split3(c))
        one = jnp.ones_like(c)
        zero = jnp.zeros_like(c)
        cq = jnp.where(sub == 0, chi, jnp.where(sub == 1, cmid, jnp.where(sub == 2, clo,
             jnp.where(sub < 6, one, zero))))
        ck = jnp.where(sub < 3, one, jnp.where(sub == 3, -chi, jnp.where(sub == 4, -cmid,
             jnp.where(sub == 5, -clo, zero))))
        cq_ref[r * CUM_BLOCK:(r + 1) * CUM_BLOCK, :] = jnp.where(valid, cq, zero).astype(BF16)
        ck_ref[r * CUM_BLOCK:(r + 1) * CUM_BLOCK, :] = jnp.where(valid, ck, zero).astype(BF16)
    carry_ref[...] = carry


def _in_proj(x, g_mix, w_main, w_vt, w_f, b_f, taps, conv_b, ln_g, ln_b, *, tm, cw, fw, conv_rows):
    b, s, d = x.shape
    n_main = w_main.shape[1]
    row = lambda i, j: (i, j, 0)
    const = lambda i, j: (0, 0)
    return pl.pallas_call(
        functools.partial(_in_proj_kernel, cw=cw, fw=fw, conv_rows=conv_rows),
        grid=(b, s // tm),
        in_specs=[
            pl.BlockSpec((None, tm, d), row),
            pl.BlockSpec((1, d), const),
            pl.BlockSpec((d, n_main), const),
            pl.BlockSpec((fw, d), const),
            pl.BlockSpec((d, LANES), const),
            pl.BlockSpec((1, LANES), const),
            pl.BlockSpec((CONV_K, 8, cw), lambda i, j: (0, 0, 0)),
            pl.BlockSpec((1, cw), const),
            pl.BlockSpec((1, cw), const),
            pl.BlockSpec((1, cw), const),
        ],
        out_specs=[
            pl.BlockSpec((None, tm, cw), row),
            pl.BlockSpec((None, tm, fw), row),
            pl.BlockSpec((None, tm, fw), row),
            pl.BlockSpec((None, fw, tm), lambda i, j: (i, 0, j)),
            pl.BlockSpec((None, tm, LANES), row),
            pl.BlockSpec((None, tm, LANES), row),
        ],
        out_shape=[
            jax.ShapeDtypeStruct((b, s, cw), BF16),
            jax.ShapeDtypeStruct((b, s, fw), BF16),
            jax.ShapeDtypeStruct((b, s, fw), BF16),
            jax.ShapeDtypeStruct((b, fw, s), BF16),
            jax.ShapeDtypeStruct((b, s, LANES), BF16),
            jax.ShapeDtypeStruct((b, s, LANES), BF16),
        ],
        scratch_shapes=[pltpu.VMEM((1, LANES), F32), pltpu.VMEM((HALO + tm, cw), F32),
                        pltpu.VMEM((8, HALO + tm, cw), F32)],
        compiler_params=_params(("parallel", "arbitrary")),
        name="in_proj",
    )(x, g_mix, w_main, w_vt, w_f, b_f, taps, conv_b, ln_g, ln_b)


def _conv_pieces(win_ref, sh_ref, w_ref, cb_ref, lg_ref, lb_ref, o_ref, *, rows):
    tc = o_ref.shape[0]

    def shift_copy(r):
        def run():
            sh_ref[r, 8:HALO + tc, :] = win_ref[8 - r:HALO + tc - r, :]
        return run

    def shifted(r, start):
        src = win_ref[start:start + rows, :] if r == 0 else sh_ref[r, start:start + rows, :]
        return src.reshape(rows // 8, 8, src.shape[-1])

    def chunk(c):
        def run():
            base = HALO + c * rows
            acc = None
            for lag in range(CONV_K):
                m, r = divmod(lag, 8)
                term = shifted(r, base - 8 * m) * w_ref[CONV_K - 1 - lag]
                acc = term if acc is None else acc + term
            y = acc.reshape(rows, acc.shape[-1]) + cb_ref[...]
            mu = jnp.mean(y, axis=-1, keepdims=True)
            yc = y - mu
            yn = yc * lax.rsqrt(jnp.mean(yc * yc, axis=-1, keepdims=True) + EPS)
            yn = yn * lg_ref[...] + lb_ref[...]
            o_ref[c * rows:(c + 1) * rows, :] = (yn * jax.nn.sigmoid(yn)).astype(o_ref.dtype)
        return run

    return [shift_copy(r) for r in range(1, 8)] + [chunk(c) for c in range(tc // rows)]


def _fox_kernel(q_ref, cq_ref, k_ref, ck_ref, vt_ref, o_ref, s_ref, p_ref, *, tb):
    pair = pl.program_id(1)
    s_len = q_ref.shape[0]
    lane = lax.broadcasted_iota(jnp.int32, (tb, LANES), 1)
    key = lax.broadcasted_iota(jnp.int32, (tb, tb), 0)
    qry = lax.broadcasted_iota(jnp.int32, (tb, tb), 1)
    ones = jnp.ones((16, s_len), BF16)

    def logits(qi):
        lo, hi = qi * tb, (qi + 1) * tb
        q2 = q_ref[lo:hi, :]
        cq = cq_ref[lo:hi, :]
        zero = jnp.zeros_like(q2)
        kcat = jnp.concatenate([k_ref[0:hi, :], ck_ref[0:hi, :]], axis=-1)
        for hh in range(2):
            buf = 2 * (qi % 2) + hh
            qmask = (lane >= hh * FOX_HEAD_DIM) & (lane < (hh + 1) * FOX_HEAD_DIM)
            cmask = (lane >> 3) == pair * 2 + hh
            qcat = jnp.concatenate([jnp.where(qmask, q2, zero), jnp.where(cmask, cq, zero)], axis=-1)
            s_ref[buf, 0:hi, :] = _dot_nt(kcat, qcat)
            s_ref[buf, lo:hi, :] = jnp.where(key <= qry, s_ref[buf, lo:hi, :], -jnp.inf)

    def softmax_pv(qi):
        lo, hi = qi * tb, (qi + 1) * tb
        for hh in range(2):
            buf = 2 * (qi % 2) + hh
            m = jnp.max(s_ref[buf, 0:hi, :], axis=0, keepdims=True)
            p_ref[buf, 0:hi, :] = jnp.exp2(s_ref[buf, 0:hi, :] - m).astype(BF16)
        outs = []
        for hh in range(2):
            buf = 2 * (qi % 2) + hh
            vaug = jnp.concatenate([vt_ref[hh * FOX_HEAD_DIM:(hh + 1) * FOX_HEAD_DIM, 0:hi],
                                    ones[:, 0:hi]], axis=0)
            ot = _dot(vaug, p_ref[buf, 0:hi, :])
            outs.append(ot[0:FOX_HEAD_DIM, :] / ot[FOX_HEAD_DIM:FOX_HEAD_DIM + 1, :])
        o_ref[lo:hi, :] = jnp.concatenate(outs, axis=0).T.astype(o_ref.dtype)

    nq = s_len // tb
    logits(0)
    for qi in range(nq):
        if qi + 1 < nq:
            logits(qi + 1)
        softmax_pv(qi)


def _fox(q, k, vt, cq, ck, *, tb):
    b, s, fw = q.shape
    pairs = fw // LANES
    qspec = pl.BlockSpec((None, s, LANES), lambda i, p: (i, 0, p))
    cspec = pl.BlockSpec((None, s, LANES), lambda i, p: (i, 0, 0))
    vspec = pl.BlockSpec((None, LANES, s), lambda i, p: (i, p, 0))
    return pl.pallas_call(
        functools.partial(_fox_kernel, tb=tb),
        grid=(b, pairs),
        in_specs=[qspec, cspec, qspec, cspec, vspec],
        out_specs=qspec,
        out_shape=jax.ShapeDtypeStruct((b, s, fw), BF16),
        scratch_shapes=[pltpu.VMEM((4, s, tb), F32), pltpu.VMEM((4, s, tb), BF16)],
        compiler_params=_params(("parallel", "parallel")),
        name="fox",
    )(q, cq, k, ck, vt)


def _mix_mem_kernel(x_ref, conv_ref, att_ref, wo_ref, gx_ref, wq_ref, mk_ref, mv_ref, wmo_ref, o_ref,
                    *, n_sub):
    cw = conv_ref.shape[-1]
    tm, d = x_ref.shape
    dh = d // MEM_HEADS

    def chain(rows):
        x1 = x_ref[rows, :] + _dot(conv_ref[rows, :], wo_ref[0:cw, :]) + _dot(att_ref[rows, :], wo_ref[cw:, :])
        yield
        hx = _rms(x1, gx_ref[...]).astype(BF16)
        yield
        qm = (_dot(hx, wq_ref[...]) * (1.0 / math.sqrt(dh))).astype(BF16)
        yield
        outs = []
        for h in range(MEM_HEADS):
            s = _dot_nt(qm[:, h * dh:(h + 1) * dh], mk_ref[:, h * dh:(h + 1) * dh])
            yield
            e = jnp.exp(s - jnp.max(s, axis=-1, keepdims=True))
            p = e / jnp.sum(e, axis=-1, keepdims=True)
            yield
            outs.append(_dot(p.astype(BF16), mv_ref[:, h * dh:(h + 1) * dh]).astype(BF16))
            yield
        o = jnp.concatenate(outs, axis=-1)
        o_ref[rows, :] = x1 + _dot(o, wmo_ref[...])

    sub = tm // n_sub
    chains = [chain(slice(i * sub, (i + 1) * sub)) for i in range(n_sub)]
    live = []
    while chains or live:
        if chains:
            live.append(chains.pop(0))
        for c in list(live):
            if next(c, StopIteration) is StopIteration:
                live.remove(c)


def _mix_mem(x, conv_out, att_out, w_out, g_x, w_mq, mk, mv, w_mo, *, tm):
    b, s, d = x.shape
    cw = conv_out.shape[-1]
    fw = att_out.shape[-1]
    m = mk.shape[1]
    row = lambda i, j: (i, j, 0)
    const = lambda i, j: (0, 0)
    batch = lambda i, j: (i, 0, 0)
    return pl.pallas_call(
        functools.partial(_mix_mem_kernel, n_sub=4),
        grid=(b, s // tm),
        in_specs=[
            pl.BlockSpec((None, tm, d), row),
            pl.BlockSpec((None, tm, cw), row),
            pl.BlockSpec((None, tm, fw), row),
            pl.BlockSpec((cw + fw, d), const),
            pl.BlockSpec((1, d), const),
            pl.BlockSpec((d, d), const),
            pl.BlockSpec((None, m, d), batch),
            pl.BlockSpec((None, m, d), batch),
            pl.BlockSpec((d, d), const),
        ],
        out_specs=pl.BlockSpec((None, tm, d), row),
        out_shape=jax.ShapeDtypeStruct((b, s, d), F32),
        compiler_params=_params(("parallel", "parallel")),
        name="mix_mem",
    )(x, conv_out, att_out, w_out, g_x, w_mq, mk, mv, w_mo)


def _ffn_kernel(x_ref, gf_ref, wg_ref, wu_ref, wd_ref, gl_ref, o_ref):
    x2 = x_ref[...]
    hf = _rms(x2, gf_ref[...]).astype(BF16)
    g = _dot(hf, wg_ref[...])
    u = _dot(hf, wu_ref[...])
    act = (g * jax.nn.sigmoid(g) * u).astype(BF16)
    x3 = x2 + _dot(act, wd_ref[...])
    o_ref[...] = _rms(x3, gl_ref[...])


def _ffn(x2, g_ffn, w_g, w_u, w_down, g_final, *, tm):
    t, d = x2.shape
    dff = w_g.shape[1]
    row = lambda i: (i, 0)
    const = lambda i: (0, 0)
    return pl.pallas_call(
        _ffn_kernel,
        grid=(t // tm,),
        in_specs=[
            pl.BlockSpec((tm, d), row),
            pl.BlockSpec((1, d), const),
            pl.BlockSpec((d, dff), const),
            pl.BlockSpec((d, dff), const),
            pl.BlockSpec((dff, d), const),
            pl.BlockSpec((1, d), const),
        ],
        out_specs=pl.BlockSpec((tm, d), row),
        out_shape=jax.ShapeDtypeStruct((t, d), F32),
        compiler_params=_params(("parallel",)),
        name="ffn",
    )(x2, g_ffn, w_g, w_u, w_down, g_final)


def _tile(n, want):
    t = min(n, want)
    assert n % t == 0, (n, t)
    return t


def kernel(x, mem, g_mix, w_in, b_f, conv_w, conv_b, ln_g, ln_b, w_out, g_x, g_mem, w_mq, w_mkv,
           w_mo, g_ffn, w_gu, w_down, g_final):
    b, s, d = x.shape
    cw = conv_w.shape[-1]
    fw = FOX_HEADS * FOX_HEAD_DIM
    n_main = 2 * cw + 2 * fw
    dff = w_down.shape[1]
    assert w_in.shape[0] == 1, "single-layer block only"
    assert w_in.shape[-1] == n_main + fw + FOX_HEADS and conv_w.shape[1] == CONV_K
    assert fw % LANES == 0 and FOX_HEADS * 8 <= LANES

    mk, mv = _mem_kv(mem, g_mem.reshape(1, d), w_mkv[0].astype(BF16))

    w_main = w_in[0][:, :n_main].astype(BF16)
    w_vt = w_in[0][:, n_main:n_main + fw].T.astype(BF16)
    w_f = jnp.repeat(w_in[0][:, n_main + fw:], 8, axis=1)
    w_f = jnp.pad(w_f, ((0, 0), (0, LANES - w_f.shape[1]))).astype(BF16)
    bf = jnp.pad(jnp.repeat(b_f[0], 8), (0, LANES - FOX_HEADS * 8)).reshape(1, LANES)

    conv_taps = jnp.broadcast_to(conv_w[0][:, None, :], (CONV_K, 8, cw))
    conv_out, q, k, vt, cq, ck = _in_proj(
        x, g_mix[0].reshape(1, d), w_main, w_vt, w_f, bf, conv_taps, conv_b[0].reshape(1, cw),
        ln_g[0].reshape(1, cw), ln_b[0].reshape(1, cw), tm=_tile(s, 512), cw=cw, fw=fw, conv_rows=32)
    att_out = _fox(q, k, vt, cq, ck, tb=_tile(s, 256))
    x2 = _mix_mem(x, conv_out, att_out, w_out[0].astype(BF16), g_x[0].reshape(1, d),
                  w_mq[0].astype(BF16), mk, mv, w_mo[0].astype(BF16), tm=_tile(s, 1024))
    y = _ffn(x2.reshape(b * s, d), g_ffn[0].reshape(1, d), w_gu[0][:, :dff].astype(BF16),
             w_gu[0][:, dff:].astype(BF16), w_down[0].astype(BF16), g_final.reshape(1, d),
             tm=_tile(b * s, 512))
    return y.reshape(b, s, d)
```

```python
import functools
import math

import jax
import jax.numpy as jnp
from jax import lax
from jax.experimental import pallas as pl
from jax.experimental.pallas import tpu as pltpu

F32 = jnp.float32
BF16 = jnp.bfloat16

EPS = 1e-6
LOG2E = math.log2(math.e)
CONV_K = 31
FOX_HEADS = 8
FOX_HEAD_DIM = 64
MEM_HEADS = 4

LANES = 128
CUM_BLOCK = 128
HALO = 32
VMEM_LIMIT = 56 * 1024 * 1024


def _params(semantics):
    return pltpu.CompilerParams(dimension_semantics=semantics, vmem_limit_bytes=VMEM_LIMIT)


def _rms(x, g):
    return x * lax.rsqrt(jnp.mean(x * x, axis=-1, keepdims=True) + EPS) * g


def _split3(x):
    hi = x.astype(BF16)
    r1 = x - hi.astype(F32)
    mid = r1.astype(BF16)
    lo = (r1 - mid.astype(F32)).astype(BF16)
    return hi, mid, lo


def _dot(a, b):
    return jnp.dot(a, b, preferred_element_type=F32)


def _dot_nt(a, b):
    return lax.dot_general(a, b, (((1,), (1,)), ((), ())), preferred_element_type=F32)


def _run_staggered(chains):
    chains = list(chains)
    live = []
    while chains or live:
        if chains:
            live.append(chains.pop(0))
        for c in list(live):
            if next(c, StopIteration) is StopIteration:
                live.remove(c)


def _mem_kv_kernel(mem_ref, g_ref, w_ref, k_ref, v_ref):
    d = mem_ref.shape[-1]
    mn = _rms(mem_ref[...], g_ref[...]).astype(BF16)
    k_ref[...] = _dot(mn, w_ref[:, :d]).astype(BF16)
    v_ref[...] = _dot(mn, w_ref[:, d:]).astype(BF16)


def _mem_kv(mem, g_mem, w_mkv):
    b, m, d = mem.shape
    return pl.pallas_call(
        _mem_kv_kernel,
        grid=(b,),
        in_specs=[
            pl.BlockSpec((None, m, d), lambda i: (i, 0, 0)),
            pl.BlockSpec((1, d), lambda i: (0, 0)),
            pl.BlockSpec((d, 2 * d), lambda i: (0, 0)),
        ],
        out_specs=[
            pl.BlockSpec((None, m, d), lambda i: (i, 0, 0)),
            pl.BlockSpec((None, m, d), lambda i: (i, 0, 0)),
        ],
        out_shape=[jax.ShapeDtypeStruct((b, m, d), BF16)] * 2,
        compiler_params=_params(("parallel",)),
        name="mem_kv",
    )(mem, g_mem, w_mkv)


def _conv_tile(win_ref, sh_ref, y_ref, w_ref, cb_ref, lg_ref, lb_ref, o_ref, *, rows):
    tc = o_ref.shape[0]

    for r in range(1, 8):
        sh_ref[r, 8:HALO + tc, :] = win_ref[8 - r:HALO + tc - r, :]

    def shifted(r, start):
        src = win_ref[start:start + rows, :] if r == 0 else sh_ref[r, start:start + rows, :]
        return src.reshape(rows // 8, 8, src.shape[-1])

    for c in range(tc // rows):
        base = HALO + c * rows
        acc = None
        for lag in range(CONV_K):
            m, r = divmod(lag, 8)
            term = shifted(r, base - 8 * m) * w_ref[CONV_K - 1 - lag]
            acc = term if acc is None else acc + term
        y_ref[c * rows:(c + 1) * rows, :] = acc.reshape(rows, acc.shape[-1])

    y = y_ref[...] + cb_ref[...]
    mu = jnp.mean(y, axis=-1, keepdims=True)
    yc = y - mu
    yn = yc * lax.rsqrt(jnp.mean(yc * yc, axis=-1, keepdims=True) + EPS)
    yn = yn * lg_ref[...] + lb_ref[...]
    o_ref[...] = (yn * jax.nn.sigmoid(yn)).astype(o_ref.dtype)


def _in_proj_kernel(x_ref, g_ref, w_ref, wvt_ref, wf_ref, bf_ref, taps_ref, cb_ref, lg_ref, lb_ref,
                    conv_ref, q_ref, k_ref, vt_ref, cq_ref, ck_ref,
                    carry_ref, win_ref, sh_ref, y_ref, *, cw, fw, conv_rows):
    tm = x_ref.shape[0]

    @pl.when(pl.program_id(1) == 0)
    def _():
        carry_ref[...] = jnp.zeros_like(carry_ref)
        win_ref[0:HALO, :] = jnp.zeros((HALO, win_ref.shape[1]), F32)

    @pl.when(pl.program_id(1) > 0)
    def _():
        win_ref[0:HALO, :] = win_ref[tm:tm + HALO, :]

    h = _rms(x_ref[...], g_ref[...]).astype(BF16)

    u = _dot(h, w_ref[:, 0:cw])
    gate = _dot(h, w_ref[:, cw:2 * cw])
    win_ref[HALO:HALO + tm, :] = u * jax.nn.sigmoid(gate)
    _conv_tile(win_ref, sh_ref, y_ref, taps_ref, cb_ref, lg_ref, lb_ref, conv_ref, rows=conv_rows)

    off = 2 * cw
    scale = LOG2E / math.sqrt(FOX_HEAD_DIM)
    q_ref[...] = (_dot(h, w_ref[:, off:off + fw]) * scale).astype(BF16)
    k_ref[...] = _dot(h, w_ref[:, off + fw:off + 2 * fw]).astype(BF16)
    vt_ref[...] = _dot_nt(wvt_ref[...], h).astype(BF16)

    zf = _dot(h, wf_ref[...]) + bf_ref[...]
    logf = jnp.minimum(zf, 0.0) - jnp.log1p(jnp.exp(-jnp.abs(zf)))
    logf = logf * LOG2E

    row = lax.broadcasted_iota(jnp.int32, (CUM_BLOCK, CUM_BLOCK), 0)
    col = lax.broadcasted_iota(jnp.int32, (CUM_BLOCK, CUM_BLOCK), 1)
    tri = (col <= row).astype(BF16)

    lane = lax.broadcasted_iota(jnp.int32, (CUM_BLOCK, LANES), 1)
    sub = lane & 7
    valid = lane < FOX_HEADS * 8
    carry = carry_ref[...]
    for r in range(tm // CUM_BLOCK):
        blk = logf[r * CUM_BLOCK:(r + 1) * CUM_BLOCK, :]
        hi, mid, lo = _split3(blk)
        c = (_dot(tri, hi) + _dot(tri, mid)) + _dot(tri, lo) + carry
        carry = c[CUM_BLOCK - 1:CUM_BLOCK, :]
        chi, cmid, clo = (t.astype(F32) for t in _split3(c))
        one = jnp.ones_like(c)
        zero = jnp.zeros_like(c)
        cq = jnp.where(sub == 0, chi, jnp.where(sub == 1, cmid, jnp.where(sub == 2, clo,
             jnp.where(sub < 6, one, zero))))
        ck = jnp.where(sub < 3, one, jnp.where(sub == 3, -chi, jnp.where(sub == 4, -cmid,
             jnp.where(sub == 5, -clo, zero))))
        cq_ref[r * CUM_BLOCK:(r + 1) * CUM_BLOCK, :] = jnp.where(valid, cq, zero).astype(BF16)
        ck_ref[r * CUM_BLOCK:(r + 1) * CUM_BLOCK, :] = jnp.where(valid, ck, zero).astype(BF16)
    carry_ref[...] = carry


def _in_proj(x, g_mix, w_main, w_vt, w_f, b_f, taps, conv_b, ln_g, ln_b, *, tm, cw, fw, conv_rows):
    b, s, d = x.shape
    n_main = w_main.shape[1]
    row = lambda i, j: (i, j, 0)
    const = lambda i, j: (0, 0)
    return pl.pallas_call(
        functools.partial(_in_proj_kernel, cw=cw, fw=fw, conv_rows=conv_rows),
        grid=(b, s // tm),
        in_specs=[
            pl.BlockSpec((None, tm, d), row),
            pl.BlockSpec((1, d), const),
            pl.BlockSpec((d, n_main), const),
            pl.BlockSpec((fw, d), const),
            pl.BlockSpec((d, LANES), const),
            pl.BlockSpec((1, LANES), const),
            pl.BlockSpec((CONV_K, 8, cw), lambda i, j: (0, 0, 0)),
            pl.BlockSpec((1, cw), const),
            pl.BlockSpec((1, cw), const),
            pl.BlockSpec((1, cw), const),
        ],
        out_specs=[
            pl.BlockSpec((None, tm, cw), row),
            pl.BlockSpec((None, tm, fw), row),
            pl.BlockSpec((None, tm, fw), row),
            pl.BlockSpec((None, fw, tm), lambda i, j: (i, 0, j)),
            pl.BlockSpec((None, tm, LANES), row),
            pl.BlockSpec((None, tm, LANES), row),
        ],
        out_shape=[
            jax.ShapeDtypeStruct((b, s, cw), BF16),
            jax.ShapeDtypeStruct((b, s, fw), BF16),
            jax.ShapeDtypeStruct((b, s, fw), BF16),
            jax.ShapeDtypeStruct((b, fw, s), BF16),
            jax.ShapeDtypeStruct((b, s, LANES), BF16),
            jax.ShapeDtypeStruct((b, s, LANES), BF16),
        ],
        scratch_shapes=[pltpu.VMEM((1, LANES), F32), pltpu.VMEM((HALO + tm, cw), F32),
                        pltpu.VMEM((8, HALO + tm, cw), F32), pltpu.VMEM((tm, cw), F32)],
        compiler_params=_params(("parallel", "arbitrary")),
        name="in_proj",
    )(x, g_mix, w_main, w_vt, w_f, b_f, taps, conv_b, ln_g, ln_b)


def _fox_kernel(q_ref, cq_ref, k_ref, ck_ref, vt_ref, o_ref, s_ref, p_ref, *, tb):
    pair = pl.program_id(1)
    s_len = q_ref.shape[0]
    lane = lax.broadcasted_iota(jnp.int32, (tb, LANES), 1)
    key = lax.broadcasted_iota(jnp.int32, (tb, tb), 0)
    qry = lax.broadcasted_iota(jnp.int32, (tb, tb), 1)
    ones = jnp.ones((16, s_len), BF16)

    def logits(qi, hh):
        lo, hi = qi * tb, (qi + 1) * tb
        buf = 2 * (qi % 2) + hh
        q2 = q_ref[lo:hi, :]
        cq = cq_ref[lo:hi, :]
        zero = jnp.zeros_like(q2)
        kcat = jnp.concatenate([k_ref[0:hi, :], ck_ref[0:hi, :]], axis=-1)
        qmask = (lane >= hh * FOX_HEAD_DIM) & (lane < (hh + 1) * FOX_HEAD_DIM)
        cmask = (lane >> 3) == pair * 2 + hh
        qcat = jnp.concatenate([jnp.where(qmask, q2, zero), jnp.where(cmask, cq, zero)], axis=-1)
        s_ref[buf, 0:hi, :] = _dot_nt(kcat, qcat)
        s_ref[buf, lo:hi, :] = jnp.where(key <= qry, s_ref[buf, lo:hi, :], -jnp.inf)

    def softmax(qi, hh):
        hi = (qi + 1) * tb
        buf = 2 * (qi % 2) + hh
        m = jnp.max(s_ref[buf, 0:hi, :], axis=0, keepdims=True)
        p_ref[buf, 0:hi, :] = jnp.exp2(s_ref[buf, 0:hi, :] - m).astype(BF16)

    heads_out = {}

    def weighted_values(qi, hh):
        lo, hi = qi * tb, (qi + 1) * tb
        buf = 2 * (qi % 2) + hh
        vaug = jnp.concatenate([vt_ref[hh * FOX_HEAD_DIM:(hh + 1) * FOX_HEAD_DIM, 0:hi],
                                ones[:, 0:hi]], axis=0)
        ot = _dot(vaug, p_ref[buf, 0:hi, :])
        heads_out[hh] = ot[0:FOX_HEAD_DIM, :] / ot[FOX_HEAD_DIM:FOX_HEAD_DIM + 1, :]
        if hh == 1:
            o_ref[lo:hi, :] = jnp.concatenate([heads_out[0], heads_out[1]], axis=0).T.astype(o_ref.dtype)

    nq = s_len // tb
    for hh in range(2):
        logits(0, hh)
    for qi in range(nq):
        for hh in range(2):
            if qi + 1 < nq:
                logits(qi + 1, hh)
            softmax(qi, hh)
            if qi >= 1:
                weighted_values(qi - 1, hh)
    for hh in range(2):
        weighted_values(nq - 1, hh)


def _fox(q, k, vt, cq, ck, *, tb):
    b, s, fw = q.shape
    pairs = fw // LANES
    qspec = pl.BlockSpec((None, s, LANES), lambda i, p: (i, 0, p))
    cspec = pl.BlockSpec((None, s, LANES), lambda i, p: (i, 0, 0))
    vspec = pl.BlockSpec((None, LANES, s), lambda i, p: (i, p, 0))
    return pl.pallas_call(
        functools.partial(_fox_kernel, tb=tb),
        grid=(b, pairs),
        in_specs=[qspec, cspec, qspec, cspec, vspec],
        out_specs=qspec,
        out_shape=jax.ShapeDtypeStruct((b, s, fw), BF16),
        scratch_shapes=[pltpu.VMEM((4, s, tb), F32), pltpu.VMEM((4, s, tb), BF16)],
        compiler_params=_params(("parallel", "parallel")),
        name="fox",
    )(q, cq, k, ck, vt)


def _mix_mem_kernel(x_ref, conv_ref, att_ref, wo_ref, gx_ref, wq_ref, mk_ref, mv_ref, wmo_ref, o_ref,
                    *, n_sub):
    cw = conv_ref.shape[-1]
    tm, d = x_ref.shape
    dh = d // MEM_HEADS

    def chain(rows):
        x1 = x_ref[rows, :] + _dot(conv_ref[rows, :], wo_ref[0:cw, :]) + _dot(att_ref[rows, :], wo_ref[cw:, :])
        yield
        hx = _rms(x1, gx_ref[...]).astype(BF16)
        yield
        qm = (_dot(hx, wq_ref[...]) * (1.0 / math.sqrt(dh))).astype(BF16)
        yield
        outs = []
        for h in range(MEM_HEADS):
            s = _dot_nt(qm[:, h * dh:(h + 1) * dh], mk_ref[:, h * dh:(h + 1) * dh])
            yield
            e = jnp.exp(s - jnp.max(s, axis=-1, keepdims=True))
            p = e / jnp.sum(e, axis=-1, keepdims=True)
            yield
            outs.append(_dot(p.astype(BF16), mv_ref[:, h * dh:(h + 1) * dh]).astype(BF16))
            yield
        o = jnp.concatenate(outs, axis=-1)
        o_ref[rows, :] = x1 + _dot(o, wmo_ref[...])

    sub = tm // n_sub
    _run_staggered(chain(slice(i * sub, (i + 1) * sub)) for i in range(n_sub))


def _mix_mem(x, conv_out, att_out, w_out, g_x, w_mq, mk, mv, w_mo, *, tm, n_sub):
    b, s, d = x.shape
    cw = conv_out.shape[-1]
    fw = att_out.shape[-1]
    m = mk.shape[1]
    row = lambda i, j: (i, j, 0)
    const = lambda i, j: (0, 0)
    batch = lambda i, j: (i, 0, 0)
    return pl.pallas_call(
        functools.partial(_mix_mem_kernel, n_sub=n_sub),
        grid=(b, s // tm),
        in_specs=[
            pl.BlockSpec((None, tm, d), row),
            pl.BlockSpec((None, tm, cw), row),
            pl.BlockSpec((None, tm, fw), row),
            pl.BlockSpec((cw + fw, d), const),
            pl.BlockSpec((1, d), const),
            pl.BlockSpec((d, d), const),
            pl.BlockSpec((None, m, d), batch),
            pl.BlockSpec((None, m, d), batch),
            pl.BlockSpec((d, d), const),
        ],
        out_specs=pl.BlockSpec((None, tm, d), row),
        out_shape=jax.ShapeDtypeStruct((b, s, d), F32),
        compiler_params=_params(("parallel", "parallel")),
        name="mix_mem",
    )(x, conv_out, att_out, w_out, g_x, w_mq, mk, mv, w_mo)


def _ffn_kernel(x_ref, gf_ref, wg_ref, wu_ref, wd_ref, gl_ref, o_ref, *, n_sub):
    tm = x_ref.shape[0]

    def chain(rows):
        x2 = x_ref[rows, :]
        hf = _rms(x2, gf_ref[...]).astype(BF16)
        yield
        g = _dot(hf, wg_ref[...])
        yield
        u = _dot(hf, wu_ref[...])
        act = (g * jax.nn.sigmoid(g) * u).astype(BF16)
        yield
        x3 = x2 + _dot(act, wd_ref[...])
        yield
        o_ref[rows, :] = _rms(x3, gl_ref[...])

    sub = tm // n_sub
    _run_staggered(chain(slice(i * sub, (i + 1) * sub)) for i in range(n_sub))


def _ffn(x2, g_ffn, w_g, w_u, w_down, g_final, *, tm, n_sub):
    t, d = x2.shape
    dff = w_g.shape[1]
    row = lambda i: (i, 0)
    const = lambda i: (0, 0)
    return pl.pallas_call(
        functools.partial(_ffn_kernel, n_sub=n_sub),
        grid=(t // tm,),
        in_specs=[
            pl.BlockSpec((tm, d), row),
            pl.BlockSpec((1, d), const),
            pl.BlockSpec((d, dff), const),
            pl.BlockSpec((d, dff), const),
            pl.BlockSpec((dff, d), const),
            pl.BlockSpec((1, d), const),
        ],
        out_specs=pl.BlockSpec((tm, d), row),
        out_shape=jax.ShapeDtypeStruct((t, d), F32),
        compiler_params=_params(("parallel",)),
        name="ffn",
    )(x2, g_ffn, w_g, w_u, w_down, g_final)


def _tile(n, want):
    t = min(n, want)
    assert n % t == 0, (n, t)
    return t


def kernel(x, mem, g_mix, w_in, b_f, conv_w, conv_b, ln_g, ln_b, w_out, g_x, g_mem, w_mq, w_mkv,
           w_mo, g_ffn, w_gu, w_down, g_final):
    b, s, d = x.shape
    cw = conv_w.shape[-1]
    fw = FOX_HEADS * FOX_HEAD_DIM
    n_main = 2 * cw + 2 * fw
    dff = w_down.shape[1]
    assert w_in.shape[0] == 1, "single-layer block only"
    assert w_in.shape[-1] == n_main + fw + FOX_HEADS and conv_w.shape[1] == CONV_K
    assert fw % LANES == 0 and FOX_HEADS * 8 <= LANES

    mk, mv = _mem_kv(mem, g_mem.reshape(1, d), w_mkv[0].astype(BF16))

    w_main = w_in[0][:, :n_main].astype(BF16)
    w_vt = w_in[0][:, n_main:n_main + fw].T.astype(BF16)
    w_f = jnp.repeat(w_in[0][:, n_main + fw:], 8, axis=1)
    w_f = jnp.pad(w_f, ((0, 0), (0, LANES - w_f.shape[1]))).astype(BF16)
    bf = jnp.pad(jnp.repeat(b_f[0], 8), (0, LANES - FOX_HEADS * 8)).reshape(1, LANES)

    conv_taps = jnp.broadcast_to(conv_w[0][:, None, :], (CONV_K, 8, cw))
    conv_out, q, k, vt, cq, ck = _in_proj(
        x, g_mix[0].reshape(1, d), w_main, w_vt, w_f, bf, conv_taps, conv_b[0].reshape(1, cw),
        ln_g[0].reshape(1, cw), ln_b[0].reshape(1, cw), tm=_tile(s, 512), cw=cw, fw=fw, conv_rows=16)
    att_out = _fox(q, k, vt, cq, ck, tb=_tile(s, 256))
    tm_mix = _tile(s, 1024)
    x2 = _mix_mem(x, conv_out, att_out, w_out[0].astype(BF16), g_x[0].reshape(1, d),
                  w_mq[0].astype(BF16), mk, mv, w_mo[0].astype(BF16), tm=tm_mix, n_sub=tm_mix // 256)
    y = _ffn(x2.reshape(b * s, d), g_ffn[0].reshape(1, d), w_gu[0][:, :dff].astype(BF16),
             w_gu[0][:, dff:].astype(BF16), w_down[0].astype(BF16), g_final.reshape(1, d),
             tm=_tile(b * s, 512), n_sub=2)
    return y.reshape(b, s, d)
```

```python
import functools
import math

import jax
import jax.numpy as jnp
from jax import lax
from jax.experimental import pallas as pl
from jax.experimental.pallas import tpu as pltpu

F32 = jnp.float32
BF16 = jnp.bfloat16

EPS = 1e-6
LOG2E = math.log2(math.e)
CONV_K = 31
FOX_HEADS = 8
FOX_HEAD_DIM = 64
MEM_HEADS = 4

LANES = 128
CUM_BLOCK = 128
HALO = 32
VMEM_LIMIT = 56 * 1024 * 1024


def _params(semantics):
    return pltpu.CompilerParams(dimension_semantics=semantics, vmem_limit_bytes=VMEM_LIMIT)


def _rms(x, g):
    return x * lax.rsqrt(jnp.mean(x * x, axis=-1, keepdims=True) + EPS) * g


def _split3(x):
    hi = x.astype(BF16)
    r1 = x - hi.astype(F32)
    mid = r1.astype(BF16)
    lo = (r1 - mid.astype(F32)).astype(BF16)
    return hi, mid, lo


def _dot(a, b):
    return jnp.dot(a, b, preferred_element_type=F32)


def _dot_nt(a, b):
    return lax.dot_general(a, b, (((1,), (1,)), ((), ())), preferred_element_type=F32)


def _run_staggered(chains):
    chains = list(chains)
    live = []
    while chains or live:
        if chains:
            live.append(chains.pop(0))
        for c in list(live):
            if next(c, StopIteration) is StopIteration:
                live.remove(c)


def _cast_weights_once(grid_rank, *pairs):
    first = pl.program_id(0) == 0
    for axis in range(1, grid_rank):
        first = first & (pl.program_id(axis) == 0)

    @pl.when(first)
    def _():
        for src_ref, dst_ref in pairs:
            dst_ref[...] = src_ref[...].astype(BF16)


def _mem_kv_kernel(mem_ref, g_ref, w32_ref, k_ref, v_ref, w_ref):
    _cast_weights_once(1, (w32_ref, w_ref))
    d = mem_ref.shape[-1]
    mn = _rms(mem_ref[...], g_ref[...]).astype(BF16)
    k_ref[...] = _dot(mn, w_ref[:, :d]).astype(BF16)
    v_ref[...] = _dot(mn, w_ref[:, d:]).astype(BF16)


def _mem_kv(mem, g_mem, w_mkv):
    b, m, d = mem.shape
    return pl.pallas_call(
        _mem_kv_kernel,
        grid=(b,),
        in_specs=[
            pl.BlockSpec((None, m, d), lambda i: (i, 0, 0)),
            pl.BlockSpec((1, d), lambda i: (0, 0)),
            pl.BlockSpec((None, d, 2 * d), lambda i: (0, 0, 0)),
        ],
        out_specs=[
            pl.BlockSpec((None, m, d), lambda i: (i, 0, 0)),
            pl.BlockSpec((None, m, d), lambda i: (i, 0, 0)),
        ],
        out_shape=[jax.ShapeDtypeStruct((b, m, d), BF16)] * 2,
        scratch_shapes=[pltpu.VMEM((d, 2 * d), BF16)],
        compiler_params=_params(("arbitrary",)),
        name="mem_kv",
    )(mem, g_mem, w_mkv)


def _conv_tile(win_ref, sh_ref, y_ref, w_ref, cb_ref, lg_ref, lb_ref, o_ref, *, rows):
    tc = o_ref.shape[0]

    for r in range(1, 8):
        sh_ref[r, 8:HALO + tc, :] = win_ref[8 - r:HALO + tc - r, :]

    def shifted(r, start):
        src = win_ref[start:start + rows, :] if r == 0 else sh_ref[r, start:start + rows, :]
        return src.reshape(rows // 8, 8, src.shape[-1])

    for c in range(tc // rows):
        base = HALO + c * rows
        acc = None
        for lag in range(CONV_K):
            m, r = divmod(lag, 8)
            term = shifted(r, base - 8 * m) * w_ref[CONV_K - 1 - lag]
            acc = term if acc is None else acc + term
        y_ref[c * rows:(c + 1) * rows, :] = acc.reshape(rows, acc.shape[-1])

    y = y_ref[...] + cb_ref[...]
    mu = jnp.mean(y, axis=-1, keepdims=True)
    yc = y - mu
    yn = yc * lax.rsqrt(jnp.mean(yc * yc, axis=-1, keepdims=True) + EPS)
    yn = yn * lg_ref[...] + lb_ref[...]
    o_ref[...] = (yn * jax.nn.sigmoid(yn)).astype(o_ref.dtype)


def _in_proj_kernel(x_ref, g_ref, w32_ref, wvt_ref, wf_ref, bf_ref, taps_ref, cb_ref, lg_ref, lb_ref,
                    conv_ref, q_ref, k_ref, vt_ref, cq_ref, ck_ref,
                    carry_ref, win_ref, sh_ref, y_ref, w_ref, *, cw, fw, conv_rows):
    tm = x_ref.shape[0]
    _cast_weights_once(2, (w32_ref, w_ref))

    @pl.when(pl.program_id(1) == 0)
    def _():
        carry_ref[...] = jnp.zeros_like(carry_ref)
        win_ref[0:HALO, :] = jnp.zeros((HALO, win_ref.shape[1]), F32)

    @pl.when(pl.program_id(1) > 0)
    def _():
        win_ref[0:HALO, :] = win_ref[tm:tm + HALO, :]

    h = _rms(x_ref[...], g_ref[...]).astype(BF16)

    u = _dot(h, w_ref[:, 0:cw])
    gate = _dot(h, w_ref[:, cw:2 * cw])
    win_ref[HALO:HALO + tm, :] = u * jax.nn.sigmoid(gate)
    _conv_tile(win_ref, sh_ref, y_ref, taps_ref, cb_ref, lg_ref, lb_ref, conv_ref, rows=conv_rows)

    off = 2 * cw
    scale = LOG2E / math.sqrt(FOX_HEAD_DIM)
    q_ref[...] = (_dot(h, w_ref[:, off:off + fw]) * scale).astype(BF16)
    k_ref[...] = _dot(h, w_ref[:, off + fw:off + 2 * fw]).astype(BF16)
    vt_ref[...] = _dot_nt(wvt_ref[...], h).astype(BF16)

    zf = _dot(h, wf_ref[...]) + bf_ref[...]
    logf = jnp.minimum(zf, 0.0) - jnp.log1p(jnp.exp(-jnp.abs(zf)))
    logf = logf * LOG2E

    row = lax.broadcasted_iota(jnp.int32, (CUM_BLOCK, CUM_BLOCK), 0)
    col = lax.broadcasted_iota(jnp.int32, (CUM_BLOCK, CUM_BLOCK), 1)
    tri = (col <= row).astype(BF16)

    lane = lax.broadcasted_iota(jnp.int32, (CUM_BLOCK, LANES), 1)
    sub = lane & 7
    valid = lane < FOX_HEADS * 8
    carry = carry_ref[...]
    for r in range(tm // CUM_BLOCK):
        blk = logf[r * CUM_BLOCK:(r + 1) * CUM_BLOCK, :]
        hi, mid, lo = _split3(blk)
        c = (_dot(tri, hi) + _dot(tri, mid)) + _dot(tri, lo) + carry
        carry = c[CUM_BLOCK - 1:CUM_BLOCK, :]
        chi, cmid, clo = (t.astype(F32) for t in _split3(c))
        one = jnp.ones_like(c)
        zero = jnp.zeros_like(c)
        cq = jnp.where(sub == 0, chi, jnp.where(sub == 1, cmid, jnp.where(sub == 2, clo,
             jnp.where(sub < 6, one, zero))))
        ck = jnp.where(sub < 3, one, jnp.where(sub == 3, -chi, jnp.where(sub == 4, -cmid,
             jnp.where(sub == 5, -clo, zero))))
        cq_ref[r * CUM_BLOCK:(r + 1) * CUM_BLOCK, :] = jnp.where(valid, cq, zero).astype(BF16)
        ck_ref[r * CUM_BLOCK:(r + 1) * CUM_BLOCK, :] = jnp.where(valid, ck, zero).astype(BF16)
    carry_ref[...] = carry


def _in_proj(x, g_mix, w_in, w_vt, w_f, b_f, taps, conv_b, ln_g, ln_b, *, tm, n_main, cw, fw, conv_rows):
    b, s, d = x.shape
    row = lambda i, j: (i, j, 0)
    const = lambda i, j: (0, 0)
    return pl.pallas_call(
        functools.partial(_in_proj_kernel, cw=cw, fw=fw, conv_rows=conv_rows),
        grid=(b, s // tm),
        in_specs=[
            pl.BlockSpec((None, tm, d), row),
            pl.BlockSpec((1, d), const),
            pl.BlockSpec((None, d, n_main), lambda i, j: (0, 0, 0)),
            pl.BlockSpec((fw, d), const),
            pl.BlockSpec((d, LANES), const),
            pl.BlockSpec((1, LANES), const),
            pl.BlockSpec((CONV_K, 8, cw), lambda i, j: (0, 0, 0)),
            pl.BlockSpec((1, cw), const),
            pl.BlockSpec((1, cw), const),
            pl.BlockSpec((1, cw), const),
        ],
        out_specs=[
            pl.BlockSpec((None, tm, cw), row),
            pl.BlockSpec((None, tm, fw), row),
            pl.BlockSpec((None, tm, fw), row),
            pl.BlockSpec((None, fw, tm), lambda i, j: (i, 0, j)),
            pl.BlockSpec((None, tm, LANES), row),
            pl.BlockSpec((None, tm, LANES), row),
        ],
        out_shape=[
            jax.ShapeDtypeStruct((b, s, cw), BF16),
            jax.ShapeDtypeStruct((b, s, fw), BF16),
            jax.ShapeDtypeStruct((b, s, fw), BF16),
            jax.ShapeDtypeStruct((b, fw, s), BF16),
            jax.ShapeDtypeStruct((b, s, LANES), BF16),
            jax.ShapeDtypeStruct((b, s, LANES), BF16),
        ],
        scratch_shapes=[pltpu.VMEM((1, LANES), F32), pltpu.VMEM((HALO + tm, cw), F32),
                        pltpu.VMEM((8, HALO + tm, cw), F32), pltpu.VMEM((tm, cw), F32),
                        pltpu.VMEM((d, n_main), BF16)],
        compiler_params=_params(("arbitrary", "arbitrary")),
        name="in_proj",
    )(x, g_mix, w_in, w_vt, w_f, b_f, taps, conv_b, ln_g, ln_b)


def _fox_kernel(q_ref, cq_ref, k_ref, ck_ref, vt_ref, o_ref, s_ref, p_ref, *, tb):
    pair = pl.program_id(1)
    s_len = q_ref.shape[0]
    lane = lax.broadcasted_iota(jnp.int32, (tb, LANES), 1)
    key = lax.broadcasted_iota(jnp.int32, (tb, tb), 0)
    qry = lax.broadcasted_iota(jnp.int32, (tb, tb), 1)
    ones = jnp.ones((16, s_len), BF16)

    def logits(qi, hh):
        lo, hi = qi * tb, (qi + 1) * tb
        buf = 2 * (qi % 2) + hh
        q2 = q_ref[lo:hi, :]
        cq = cq_ref[lo:hi, :]
        zero = jnp.zeros_like(q2)
        kcat = jnp.concatenate([k_ref[0:hi, :], ck_ref[0:hi, :]], axis=-1)
        qmask = (lane >= hh * FOX_HEAD_DIM) & (lane < (hh + 1) * FOX_HEAD_DIM)
        cmask = (lane >> 3) == pair * 2 + hh
        qcat = jnp.concatenate([jnp.where(qmask, q2, zero), jnp.where(cmask, cq, zero)], axis=-1)
        s_ref[buf, 0:hi, :] = _dot_nt(kcat, qcat)
        s_ref[buf, lo:hi, :] = jnp.where(key <= qry, s_ref[buf, lo:hi, :], -jnp.inf)

    def softmax(qi, hh):
        hi = (qi + 1) * tb
        buf = 2 * (qi % 2) + hh
        m = jnp.max(s_ref[buf, 0:hi, :], axis=0, keepdims=True)
        p_ref[buf, 0:hi, :] = jnp.exp2(s_ref[buf, 0:hi, :] - m).astype(BF16)

    heads_out = {}

    def weighted_values(qi, hh):
        lo, hi = qi * tb, (qi + 1) * tb
        buf = 2 * (qi % 2) + hh
        vaug = jnp.concatenate([vt_ref[hh * FOX_HEAD_DIM:(hh + 1) * FOX_HEAD_DIM, 0:hi],
                                ones[:, 0:hi]], axis=0)
        ot = _dot(vaug, p_ref[buf, 0:hi, :])
        heads_out[hh] = ot[0:FOX_HEAD_DIM, :] / ot[FOX_HEAD_DIM:FOX_HEAD_DIM + 1, :]
        if hh == 1:
            o_ref[lo:hi, :] = jnp.concatenate([heads_out[0], heads_out[1]], axis=0).T.astype(o_ref.dtype)

    nq = s_len // tb
    for hh in range(2):
        logits(0, hh)
    for qi in range(nq):
        for hh in range(2):
            if qi + 1 < nq:
                logits(qi + 1, hh)
            softmax(qi, hh)
            if qi >= 1:
                weighted_values(qi - 1, hh)
    for hh in range(2):
        weighted_values(nq - 1, hh)


def _fox(q, k, vt, cq, ck, *, tb):
    b, s, fw = q.shape
    pairs = fw // LANES
    qspec = pl.BlockSpec((None, s, LANES), lambda i, p: (i, 0, p))
    cspec = pl.BlockSpec((None, s, LANES), lambda i, p: (i, 0, 0))
    vspec = pl.BlockSpec((None, LANES, s), lambda i, p: (i, p, 0))
    return pl.pallas_call(
        functools.partial(_fox_kernel, tb=tb),
        grid=(b, pairs),
        in_specs=[qspec, cspec, qspec, cspec, vspec],
        out_specs=qspec,
        out_shape=jax.ShapeDtypeStruct((b, s, fw), BF16),
        scratch_shapes=[pltpu.VMEM((4, s, tb), F32), pltpu.VMEM((4, s, tb), BF16)],
        compiler_params=_params(("parallel", "parallel")),
        name="fox",
    )(q, cq, k, ck, vt)


def _mix_mem_kernel(x_ref, conv_ref, att_ref, wo32_ref, gx_ref, wq32_ref, mk_ref, mv_ref, wmo32_ref,
                    o_ref, wo_ref, wq_ref, wmo_ref, *, n_sub):
    _cast_weights_once(2, (wo32_ref, wo_ref), (wq32_ref, wq_ref), (wmo32_ref, wmo_ref))
    cw = conv_ref.shape[-1]
    tm, d = x_ref.shape
    dh = d // MEM_HEADS

    def chain(rows):
        x1 = x_ref[rows, :] + _dot(conv_ref[rows, :], wo_ref[0:cw, :]) + _dot(att_ref[rows, :], wo_ref[cw:, :])
        yield
        hx = _rms(x1, gx_ref[...]).astype(BF16)
        yield
        qm = (_dot(hx, wq_ref[...]) * (1.0 / math.sqrt(dh))).astype(BF16)
        yield
        outs = []
        for h in range(MEM_HEADS):
            s = _dot_nt(qm[:, h * dh:(h + 1) * dh], mk_ref[:, h * dh:(h + 1) * dh])
            yield
            e = jnp.exp(s - jnp.max(s, axis=-1, keepdims=True))
            p = e / jnp.sum(e, axis=-1, keepdims=True)
            yield
            outs.append(_dot(p.astype(BF16), mv_ref[:, h * dh:(h + 1) * dh]).astype(BF16))
            yield
        o = jnp.concatenate(outs, axis=-1)
        o_ref[rows, :] = x1 + _dot(o, wmo_ref[...])

    sub = tm // n_sub
    _run_staggered(chain(slice(i * sub, (i + 1) * sub)) for i in range(n_sub))


def _mix_mem(x, conv_out, att_out, w_out, g_x, w_mq, mk, mv, w_mo, *, tm, n_sub):
    b, s, d = x.shape
    cw = conv_out.shape[-1]
    fw = att_out.shape[-1]
    m = mk.shape[1]
    row = lambda i, j: (i, j, 0)
    const = lambda i, j: (0, 0)
    first = lambda i, j: (0, 0, 0)
    batch = lambda i, j: (i, 0, 0)
    return pl.pallas_call(
        functools.partial(_mix_mem_kernel, n_sub=n_sub),
        grid=(b, s // tm),
        in_specs=[
            pl.BlockSpec((None, tm, d), row),
            pl.BlockSpec((None, tm, cw), row),
            pl.BlockSpec((None, tm, fw), row),
            pl.BlockSpec((None, cw + fw, d), first),
            pl.BlockSpec((1, d), const),
            pl.BlockSpec((None, d, d), first),
            pl.BlockSpec((None, m, d), batch),
            pl.BlockSpec((None, m, d), batch),
            pl.BlockSpec((None, d, d), first),
        ],
        out_specs=pl.BlockSpec((None, tm, d), row),
        out_shape=jax.ShapeDtypeStruct((b, s, d), F32),
        scratch_shapes=[pltpu.VMEM((cw + fw, d), BF16), pltpu.VMEM((d, d), BF16), pltpu.VMEM((d, d), BF16)],
        compiler_params=_params(("arbitrary", "arbitrary")),
        name="mix_mem",
    )(x, conv_out, att_out, w_out, g_x, w_mq, mk, mv, w_mo)


def _ffn_kernel(x_ref, gf_ref, wg_ref, wu_ref, wd_ref, gl_ref, o_ref, *, n_sub):
    tm = x_ref.shape[0]

    def chain(rows):
        x2 = x_ref[rows, :]
        hf = _rms(x2, gf_ref[...]).astype(BF16)
        yield
        g = _dot(hf, wg_ref[...])
        yield
        u = _dot(hf, wu_ref[...])
        act = (g * jax.nn.sigmoid(g) * u).astype(BF16)
        yield
        x3 = x2 + _dot(act, wd_ref[...])
        yield
        o_ref[rows, :] = _rms(x3, gl_ref[...])

    sub = tm // n_sub
    _run_staggered(chain(slice(i * sub, (i + 1) * sub)) for i in range(n_sub))


def _ffn(x2, g_ffn, w_g, w_u, w_down, g_final, *, tm, n_sub):
    t, d = x2.shape
    dff = w_g.shape[1]
    row = lambda i: (i, 0)
    const = lambda i: (0, 0)
    return pl.pallas_call(
        functools.partial(_ffn_kernel, n_sub=n_sub),
        grid=(t // tm,),
        in_specs=[
            pl.BlockSpec((tm, d), row),
            pl.BlockSpec((1, d), const),
            pl.BlockSpec((d, dff), const),
            pl.BlockSpec((d, dff), const),
            pl.BlockSpec((dff, d), const),
            pl.BlockSpec((1, d), const),
        ],
        out_specs=pl.BlockSpec((tm, d), row),
        out_shape=jax.ShapeDtypeStruct((t, d), F32),
        compiler_params=_params(("parallel",)),
        name="ffn",
    )(x2, g_ffn, w_g, w_u, w_down, g_final)


def _tile(n, want):
    t = min(n, want)
    assert n % t == 0, (n, t)
    return t


def kernel(x, mem, g_mix, w_in, b_f, conv_w, conv_b, ln_g, ln_b, w_out, g_x, g_mem, w_mq, w_mkv,
           w_mo, g_ffn, w_gu, w_down, g_final):
    b, s, d = x.shape
    cw = conv_w.shape[-1]
    fw = FOX_HEADS * FOX_HEAD_DIM
    n_main = 2 * cw + 2 * fw
    dff = w_down.shape[1]
    assert w_in.shape[0] == 1, "single-layer block only"
    assert w_in.shape[-1] == n_main + fw + FOX_HEADS and conv_w.shape[1] == CONV_K
    assert fw % LANES == 0 and FOX_HEADS * 8 <= LANES

    mk, mv = _mem_kv(mem, g_mem.reshape(1, d), w_mkv)

    w_vt = w_in[0][:, n_main:n_main + fw].T.astype(BF16)
    w_f = jnp.repeat(w_in[0][:, n_main + fw:], 8, axis=1)
    w_f = jnp.pad(w_f, ((0, 0), (0, LANES - w_f.shape[1]))).astype(BF16)
    bf = jnp.pad(jnp.repeat(b_f[0], 8), (0, LANES - FOX_HEADS * 8)).reshape(1, LANES)

    conv_taps = jnp.broadcast_to(conv_w[0][:, None, :], (CONV_K, 8, cw))
    conv_out, q, k, vt, cq, ck = _in_proj(
        x, g_mix[0].reshape(1, d), w_in, w_vt, w_f, bf, conv_taps, conv_b[0].reshape(1, cw),
        ln_g[0].reshape(1, cw), ln_b[0].reshape(1, cw), tm=_tile(s, 512), n_main=n_main, cw=cw, fw=fw,
        conv_rows=16)
    att_out = _fox(q, k, vt, cq, ck, tb=_tile(s, 256))
    tm_mix = _tile(s, 1024)
    x2 = _mix_mem(x, conv_out, att_out, w_out, g_x[0].reshape(1, d), w_mq, mk, mv, w_mo,
                  tm=tm_mix, n_sub=tm_mix // 256)
    y = _ffn(x2.reshape(b * s, d), g_ffn[0].reshape(1, d), w_gu[0][:, :dff].astype(BF16),
             w_gu[0][:, dff:].astype(BF16), w_down[0].astype(BF16), g_final.reshape(1, d),
             tm=_tile(b * s, 1024), n_sub=4)
    return y.reshape(b, s, d)
```

```python
import functools
import math

import jax
import jax.numpy as jnp
from jax import lax
from jax.experimental import pallas as pl
from jax.experimental.pallas import tpu as pltpu

F32 = jnp.float32
BF16 = jnp.bfloat16

EPS = 1e-6
LOG2E = math.log2(math.e)
CONV_K = 31
FOX_HEADS = 8
FOX_HEAD_DIM = 64
MEM_HEADS = 4

LANES = 128
BF16_SUBLANES = 16
CUM_BLOCK = 128
HALO = 32
VMEM_LIMIT = 56 * 1024 * 1024


def _params(semantics):
    return pltpu.CompilerParams(dimension_semantics=semantics, vmem_limit_bytes=VMEM_LIMIT)


def _rms(x, g):
    return x * lax.rsqrt(jnp.mean(x * x, axis=-1, keepdims=True) + EPS) * g


def _split3(x):
    hi = x.astype(BF16)
    r1 = x - hi.astype(F32)
    mid = r1.astype(BF16)
    lo = (r1 - mid.astype(F32)).astype(BF16)
    return hi, mid, lo


def _dot(a, b):
    return jnp.dot(a, b, preferred_element_type=F32)


def _dot_nt(a, b):
    return lax.dot_general(a, b, (((1,), (1,)), ((), ())), preferred_element_type=F32)


def _run_staggered(chains):
    chains = list(chains)
    live = []
    while chains or live:
        if chains:
            live.append(chains.pop(0))
        for c in list(live):
            if next(c, StopIteration) is StopIteration:
                live.remove(c)


def _cast_weights_once(grid_rank, *pairs):
    first = pl.program_id(0) == 0
    for axis in range(1, grid_rank):
        first = first & (pl.program_id(axis) == 0)

    @pl.when(first)
    def _():
        for src_ref, dst_ref in pairs:
            dst_ref[...] = src_ref[...].astype(BF16)


def _mem_kv_kernel(mem_ref, g_ref, w32_ref, k_ref, v_ref, w_ref):
    _cast_weights_once(1, (w32_ref, w_ref))
    d = mem_ref.shape[-1]
    mn = _rms(mem_ref[...], g_ref[...]).astype(BF16)
    k_ref[...] = _dot(mn, w_ref[:, :d]).astype(BF16)
    v_ref[...] = _dot(mn, w_ref[:, d:]).astype(BF16)


def _mem_kv(mem, g_mem, w_mkv):
    b, m, d = mem.shape
    return pl.pallas_call(
        _mem_kv_kernel,
        grid=(b,),
        in_specs=[
            pl.BlockSpec((None, m, d), lambda i: (i, 0, 0)),
            pl.BlockSpec((1, d), lambda i: (0, 0)),
            pl.BlockSpec((None, d, 2 * d), lambda i: (0, 0, 0)),
        ],
        out_specs=[
            pl.BlockSpec((None, m, d), lambda i: (i, 0, 0)),
            pl.BlockSpec((None, m, d), lambda i: (i, 0, 0)),
        ],
        out_shape=[jax.ShapeDtypeStruct((b, m, d), BF16)] * 2,
        scratch_shapes=[pltpu.VMEM((d, 2 * d), BF16)],
        compiler_params=_params(("arbitrary",)),
        name="mem_kv",
    )(mem, g_mem, w_mkv)


def _conv_tile(win_ref, sh_ref, y_ref, w_ref, cb_ref, lg_ref, lb_ref, o_ref, *, rows):
    tc = o_ref.shape[0]

    for r in range(1, 8):
        sh_ref[r, 8:HALO + tc, :] = win_ref[8 - r:HALO + tc - r, :]

    def shifted(r, start):
        src = win_ref[start:start + rows, :] if r == 0 else sh_ref[r, start:start + rows, :]
        return src.reshape(rows // 8, 8, src.shape[-1])

    for c in range(tc // rows):
        base = HALO + c * rows
        acc = None
        for lag in range(CONV_K):
            m, r = divmod(lag, 8)
            term = shifted(r, base - 8 * m) * w_ref[CONV_K - 1 - lag]
            acc = term if acc is None else acc + term
        y_ref[c * rows:(c + 1) * rows, :] = acc.reshape(rows, acc.shape[-1])

    y = y_ref[...] + cb_ref[...]
    mu = jnp.mean(y, axis=-1, keepdims=True)
    yc = y - mu
    yn = yc * lax.rsqrt(jnp.mean(yc * yc, axis=-1, keepdims=True) + EPS)
    yn = yn * lg_ref[...] + lb_ref[...]
    o_ref[...] = (yn * jax.nn.sigmoid(yn)).astype(o_ref.dtype)


def _in_proj_kernel(x_ref, g_ref, w32_ref, wv32_ref, wf_ref, bf_ref, taps_ref, cb_ref, lg_ref, lb_ref,
                    conv_ref, q_ref, k_ref, vt_ref, cq_ref, ck_ref,
                    carry_ref, win_ref, sh_ref, y_ref, w_ref, wvt_ref, *, cw, fw, conv_rows):
    tm = x_ref.shape[0]
    _cast_weights_once(2, (w32_ref, w_ref))

    @pl.when((pl.program_id(0) == 0) & (pl.program_id(1) == 0))
    def _():
        wvt_ref[...] = wv32_ref[...].T.astype(BF16)

    @pl.when(pl.program_id(1) == 0)
    def _():
        carry_ref[...] = jnp.zeros_like(carry_ref)
        win_ref[0:HALO, :] = jnp.zeros((HALO, win_ref.shape[1]), F32)

    @pl.when(pl.program_id(1) > 0)
    def _():
        win_ref[0:HALO, :] = win_ref[tm:tm + HALO, :]

    h = _rms(x_ref[...], g_ref[...]).astype(BF16)

    u = _dot(h, w_ref[:, 0:cw])
    gate = _dot(h, w_ref[:, cw:2 * cw])
    win_ref[HALO:HALO + tm, :] = u * jax.nn.sigmoid(gate)
    _conv_tile(win_ref, sh_ref, y_ref, taps_ref, cb_ref, lg_ref, lb_ref, conv_ref, rows=conv_rows)

    off = 2 * cw
    scale = LOG2E / math.sqrt(FOX_HEAD_DIM)
    q_ref[...] = (_dot(h, w_ref[:, off:off + fw]) * scale).astype(BF16)
    k_ref[...] = _dot(h, w_ref[:, off + fw:off + 2 * fw]).astype(BF16)
    vt_ref[...] = _dot_nt(wvt_ref[...], h).astype(BF16)

    zf = _dot(h, wf_ref[...]) + bf_ref[...]
    logf = jnp.minimum(zf, 0.0) - jnp.log1p(jnp.exp(-jnp.abs(zf)))
    logf = logf * LOG2E

    row = lax.broadcasted_iota(jnp.int32, (CUM_BLOCK, CUM_BLOCK), 0)
    col = lax.broadcasted_iota(jnp.int32, (CUM_BLOCK, CUM_BLOCK), 1)
    tri = (col <= row).astype(BF16)

    lane = lax.broadcasted_iota(jnp.int32, (CUM_BLOCK, LANES), 1)
    sub = lane & 7
    valid = lane < FOX_HEADS * 8
    carry = carry_ref[...]
    for r in range(tm // CUM_BLOCK):
        blk = logf[r * CUM_BLOCK:(r + 1) * CUM_BLOCK, :]
        hi, mid, lo = _split3(blk)
        c = (_dot(tri, hi) + _dot(tri, mid)) + _dot(tri, lo) + carry
        carry = c[CUM_BLOCK - 1:CUM_BLOCK, :]
        chi, cmid, clo = (t.astype(F32) for t in _split3(c))
        one = jnp.ones_like(c)
        zero = jnp.zeros_like(c)
        cq = jnp.where(sub == 0, chi, jnp.where(sub == 1, cmid, jnp.where(sub == 2, clo,
             jnp.where(sub < 6, one, zero))))
        ck = jnp.where(sub < 3, one, jnp.where(sub == 3, -chi, jnp.where(sub == 4, -cmid,
             jnp.where(sub == 5, -clo, zero))))
        cq_ref[r * CUM_BLOCK:(r + 1) * CUM_BLOCK, :] = jnp.where(valid, cq, zero).astype(BF16)
        ck_ref[r * CUM_BLOCK:(r + 1) * CUM_BLOCK, :] = jnp.where(valid, ck, zero).astype(BF16)
    carry_ref[...] = carry


def _in_proj(x, g_mix, w_in, w_f, b_f, taps, conv_b, ln_g, ln_b, *, tm, n_main, cw, fw, conv_rows):
    b, s, d = x.shape
    assert n_main % fw == 0
    row = lambda i, j: (i, j, 0)
    const = lambda i, j: (0, 0)
    return pl.pallas_call(
        functools.partial(_in_proj_kernel, cw=cw, fw=fw, conv_rows=conv_rows),
        grid=(b, s // tm),
        in_specs=[
            pl.BlockSpec((None, tm, d), row),
            pl.BlockSpec((1, d), const),
            pl.BlockSpec((None, d, n_main), lambda i, j: (0, 0, 0)),
            pl.BlockSpec((None, d, fw), lambda i, j: (0, 0, n_main // fw)),
            pl.BlockSpec((d, LANES), const),
            pl.BlockSpec((1, LANES), const),
            pl.BlockSpec((CONV_K, 8, cw), lambda i, j: (0, 0, 0)),
            pl.BlockSpec((1, cw), const),
            pl.BlockSpec((1, cw), const),
            pl.BlockSpec((1, cw), const),
        ],
        out_specs=[
            pl.BlockSpec((None, tm, cw), row),
            pl.BlockSpec((None, tm, fw), row),
            pl.BlockSpec((None, tm, fw), row),
            pl.BlockSpec((None, fw, tm), lambda i, j: (i, 0, j)),
            pl.BlockSpec((None, tm, LANES), row),
            pl.BlockSpec((None, tm, LANES), row),
        ],
        out_shape=[
            jax.ShapeDtypeStruct((b, s, cw), BF16),
            jax.ShapeDtypeStruct((b, s, fw), BF16),
            jax.ShapeDtypeStruct((b, s, fw), BF16),
            jax.ShapeDtypeStruct((b, fw, s), BF16),
            jax.ShapeDtypeStruct((b, s, LANES), BF16),
            jax.ShapeDtypeStruct((b, s, LANES), BF16),
        ],
        scratch_shapes=[pltpu.VMEM((1, LANES), F32), pltpu.VMEM((HALO + tm, cw), F32),
                        pltpu.VMEM((8, HALO + tm, cw), F32), pltpu.VMEM((tm, cw), F32),
                        pltpu.VMEM((d, n_main), BF16), pltpu.VMEM((fw, d), BF16)],
        compiler_params=_params(("arbitrary", "arbitrary")),
        name="in_proj",
    )(x, g_mix, w_in, w_in, w_f, b_f, taps, conv_b, ln_g, ln_b)


def _fox_kernel(q_ref, cq_ref, k_ref, ck_ref, vt_ref, *rest, tb, n_weights):
    w32_refs = rest[:n_weights]
    o_ref = rest[n_weights]
    w16_refs = rest[n_weights + 1:2 * n_weights + 1]
    s_ref, p_ref = rest[2 * n_weights + 1:]
    for w32_ref, w16_ref in zip(w32_refs, w16_refs):
        w16_ref[...] = w32_ref[...].astype(BF16)

    pair = pl.program_id(1)
    s_len = q_ref.shape[0]
    lane = lax.broadcasted_iota(jnp.int32, (tb, LANES), 1)
    key = lax.broadcasted_iota(jnp.int32, (tb, tb), 0)
    qry = lax.broadcasted_iota(jnp.int32, (tb, tb), 1)
    ones = jnp.ones((16, s_len), BF16)

    def logits(qi, hh):
        lo, hi = qi * tb, (qi + 1) * tb
        buf = 2 * (qi % 2) + hh
        q2 = q_ref[lo:hi, :]
        cq = cq_ref[lo:hi, :]
        zero = jnp.zeros_like(q2)
        kcat = jnp.concatenate([k_ref[0:hi, :], ck_ref[0:hi, :]], axis=-1)
        qmask = (lane >= hh * FOX_HEAD_DIM) & (lane < (hh + 1) * FOX_HEAD_DIM)
        cmask = (lane >> 3) == pair * 2 + hh
        qcat = jnp.concatenate([jnp.where(qmask, q2, zero), jnp.where(cmask, cq, zero)], axis=-1)
        s_ref[buf, 0:hi, :] = _dot_nt(kcat, qcat)
        s_ref[buf, lo:hi, :] = jnp.where(key <= qry, s_ref[buf, lo:hi, :], -jnp.inf)

    def softmax(qi, hh):
        hi = (qi + 1) * tb
        buf = 2 * (qi % 2) + hh
        m = jnp.max(s_ref[buf, 0:hi, :], axis=0, keepdims=True)
        p_ref[buf, 0:hi, :] = jnp.exp2(s_ref[buf, 0:hi, :] - m).astype(BF16)

    heads_out = {}

    def weighted_values(qi, hh):
        lo, hi = qi * tb, (qi + 1) * tb
        buf = 2 * (qi % 2) + hh
        vaug = jnp.concatenate([vt_ref[hh * FOX_HEAD_DIM:(hh + 1) * FOX_HEAD_DIM, 0:hi],
                                ones[:, 0:hi]], axis=0)
        ot = _dot(vaug, p_ref[buf, 0:hi, :])
        heads_out[hh] = ot[0:FOX_HEAD_DIM, :] / ot[FOX_HEAD_DIM:FOX_HEAD_DIM + 1, :]
        if hh == 1:
            o_ref[lo:hi, :] = jnp.concatenate([heads_out[0], heads_out[1]], axis=0).T.astype(o_ref.dtype)

    nq = s_len // tb
    for hh in range(2):
        logits(0, hh)
    for qi in range(nq):
        for hh in range(2):
            if qi + 1 < nq:
                logits(qi + 1, hh)
            softmax(qi, hh)
            if qi >= 1:
                weighted_values(qi - 1, hh)
    for hh in range(2):
        weighted_values(nq - 1, hh)


def _row_blocks(rows, steps):
    return max(n for n in range(1, steps + 1) if rows % n == 0 and (rows // n) % BF16_SUBLANES == 0)


def _fox(q, k, vt, cq, ck, ffn_weights, *, tb):
    b, s, fw = q.shape
    pairs = fw // LANES
    qspec = pl.BlockSpec((None, s, LANES), lambda i, p: (i, 0, p))
    cspec = pl.BlockSpec((None, s, LANES), lambda i, p: (i, 0, 0))
    vspec = pl.BlockSpec((None, LANES, s), lambda i, p: (i, p, 0))
    w_views, w_specs, w_shapes = [], [], []
    for w in ffn_weights:
        rows, cols = w.shape
        n = _row_blocks(rows, b * pairs)
        w_views.append(w.reshape(n, rows // n, cols))
        w_specs.append(pl.BlockSpec((None, rows // n, cols),
                                    lambda i, p, n=n: (jnp.minimum(i * pairs + p, n - 1), 0, 0)))
        w_shapes.append(jax.ShapeDtypeStruct((n, rows // n, cols), BF16))
    att, *w_bf16 = pl.pallas_call(
        functools.partial(_fox_kernel, tb=tb, n_weights=len(ffn_weights)),
        grid=(b, pairs),
        in_specs=[qspec, cspec, qspec, cspec, vspec] + w_specs,
        out_specs=[qspec] + w_specs,
        out_shape=[jax.ShapeDtypeStruct((b, s, fw), BF16)] + w_shapes,
        scratch_shapes=[pltpu.VMEM((4, s, tb), F32), pltpu.VMEM((4, s, tb), BF16)],
        compiler_params=_params(("arbitrary", "arbitrary")),
        name="fox",
    )(q, cq, k, ck, vt, *w_views)
    return att, [wb.reshape(w.shape) for wb, w in zip(w_bf16, ffn_weights)]


def _mix_mem_kernel(x_ref, conv_ref, att_ref, wo32_ref, gx_ref, wq32_ref, mk_ref, mv_ref, wmo32_ref,
                    o_ref, wo_ref, wq_ref, wmo_ref, *, n_sub):
    _cast_weights_once(2, (wo32_ref, wo_ref), (wq32_ref, wq_ref), (wmo32_ref, wmo_ref))
    cw = conv_ref.shape[-1]
    tm, d = x_ref.shape
    dh = d // MEM_HEADS

    def chain(rows):
        x1 = x_ref[rows, :] + _dot(conv_ref[rows, :], wo_ref[0:cw, :]) + _dot(att_ref[rows, :], wo_ref[cw:, :])
        yield
        hx = _rms(x1, gx_ref[...]).astype(BF16)
        yield
        qm = (_dot(hx, wq_ref[...]) * (1.0 / math.sqrt(dh))).astype(BF16)
        yield
        outs = []
        for h in range(MEM_HEADS):
            s = _dot_nt(qm[:, h * dh:(h + 1) * dh], mk_ref[:, h * dh:(h + 1) * dh])
            yield
            e = jnp.exp(s - jnp.max(s, axis=-1, keepdims=True))
            p = e / jnp.sum(e, axis=-1, keepdims=True)
            yield
            outs.append(_dot(p.astype(BF16), mv_ref[:, h * dh:(h + 1) * dh]).astype(BF16))
            yield
        o = jnp.concatenate(outs, axis=-1)
        o_ref[rows, :] = x1 + _dot(o, wmo_ref[...])

    sub = tm // n_sub
    _run_staggered(chain(slice(i * sub, (i + 1) * sub)) for i in range(n_sub))


def _mix_mem(x, conv_out, att_out, w_out, g_x, w_mq, mk, mv, w_mo, *, tm, n_sub):
    b, s, d = x.shape
    cw = conv_out.shape[-1]
    fw = att_out.shape[-1]
    m = mk.shape[1]
    row = lambda i, j: (i, j, 0)
    const = lambda i, j: (0, 0)
    first = lambda i, j: (0, 0, 0)
    batch = lambda i, j: (i, 0, 0)
    return pl.pallas_call(
        functools.partial(_mix_mem_kernel, n_sub=n_sub),
        grid=(b, s // tm),
        in_specs=[
            pl.BlockSpec((None, tm, d), row),
            pl.BlockSpec((None, tm, cw), row),
            pl.BlockSpec((None, tm, fw), row),
            pl.BlockSpec((None, cw + fw, d), first),
            pl.BlockSpec((1, d), const),
            pl.BlockSpec((None, d, d), first),
            pl.BlockSpec((None, m, d), batch),
            pl.BlockSpec((None, m, d), batch),
            pl.BlockSpec((None, d, d), first),
        ],
        out_specs=pl.BlockSpec((None, tm, d), row),
        out_shape=jax.ShapeDtypeStruct((b, s, d), F32),
        scratch_shapes=[pltpu.VMEM((cw + fw, d), BF16), pltpu.VMEM((d, d), BF16), pltpu.VMEM((d, d), BF16)],
        compiler_params=_params(("arbitrary", "arbitrary")),
        name="mix_mem",
    )(x, conv_out, att_out, w_out, g_x, w_mq, mk, mv, w_mo)


def _ffn_kernel(x_ref, gf_ref, wgu_ref, wd_ref, gl_ref, o_ref, *, n_sub):
    tm = x_ref.shape[0]
    dff = wd_ref.shape[0]

    def chain(rows):
        x2 = x_ref[rows, :]
        hf = _rms(x2, gf_ref[...]).astype(BF16)
        yield
        g = _dot(hf, wgu_ref[:, 0:dff])
        yield
        u = _dot(hf, wgu_ref[:, dff:2 * dff])
        act = (g * jax.nn.sigmoid(g) * u).astype(BF16)
        yield
        x3 = x2 + _dot(act, wd_ref[...])
        yield
        o_ref[rows, :] = _rms(x3, gl_ref[...])

    sub = tm // n_sub
    _run_staggered(chain(slice(i * sub, (i + 1) * sub)) for i in range(n_sub))


def _ffn(x2, g_ffn, w_gu, w_down, g_final, *, tm, n_sub):
    t, d = x2.shape
    dff = w_down.shape[0]
    assert w_gu.shape == (d, 2 * dff) and dff % LANES == 0
    row = lambda i: (i, 0)
    const = lambda i: (0, 0)
    return pl.pallas_call(
        functools.partial(_ffn_kernel, n_sub=n_sub),
        grid=(t // tm,),
        in_specs=[
            pl.BlockSpec((tm, d), row),
            pl.BlockSpec((1, d), const),
            pl.BlockSpec((d, 2 * dff), const),
            pl.BlockSpec((dff, d), const),
            pl.BlockSpec((1, d), const),
        ],
        out_specs=pl.BlockSpec((tm, d), row),
        out_shape=jax.ShapeDtypeStruct((t, d), F32),
        compiler_params=_params(("parallel",)),
        name="ffn",
    )(x2, g_ffn, w_gu, w_down, g_final)


def _tile(n, want):
    t = min(n, want)
    assert n % t == 0, (n, t)
    return t


def kernel(x, mem, g_mix, w_in, b_f, conv_w, conv_b, ln_g, ln_b, w_out, g_x, g_mem, w_mq, w_mkv,
           w_mo, g_ffn, w_gu, w_down, g_final):
    b, s, d = x.shape
    cw = conv_w.shape[-1]
    fw = FOX_HEADS * FOX_HEAD_DIM
    n_main = 2 * cw + 2 * fw
    assert w_in.shape[0] == 1, "single-layer block only"
    assert w_in.shape[-1] == n_main + fw + FOX_HEADS and conv_w.shape[1] == CONV_K
    assert fw % LANES == 0 and FOX_HEADS * 8 <= LANES

    mk, mv = _mem_kv(mem, g_mem.reshape(1, d), w_mkv)

    w_f = jnp.repeat(w_in[0][:, n_main + fw:], 8, axis=1)
    w_f = jnp.pad(w_f, ((0, 0), (0, LANES - w_f.shape[1]))).astype(BF16)
    bf = jnp.pad(jnp.repeat(b_f[0], 8), (0, LANES - FOX_HEADS * 8)).reshape(1, LANES)

    conv_taps = jnp.broadcast_to(conv_w[0][:, None, :], (CONV_K, 8, cw))
    conv_out, q, k, vt, cq, ck = _in_proj(
        x, g_mix[0].reshape(1, d), w_in, w_f, bf, conv_taps, conv_b[0].reshape(1, cw),
        ln_g[0].reshape(1, cw), ln_b[0].reshape(1, cw), tm=_tile(s, 512), n_main=n_main, cw=cw, fw=fw,
        conv_rows=16)
    att_out, (w_gu_bf16, w_down_bf16) = _fox(q, k, vt, cq, ck, [w_gu[0], w_down[0]], tb=_tile(s, 256))
    tm_mix = _tile(s, 1024)
    x2 = _mix_mem(x, conv_out, att_out, w_out, g_x[0].reshape(1, d), w_mq, mk, mv, w_mo,
                  tm=tm_mix, n_sub=tm_mix // 256)
    y = _ffn(x2.reshape(b * s, d), g_ffn[0].reshape(1, d), w_gu_bf16, w_down_bf16,
             g_final.reshape(1, d), tm=_tile(b * s, 1024), n_sub=4)
    return y.reshape(b, s, d)
```

```python
import functools
import math

import jax
import jax.numpy as jnp
from jax import lax
from jax.experimental import pallas as pl
from jax.experimental.pallas import tpu as pltpu

F32 = jnp.float32
BF16 = jnp.bfloat16

EPS = 1e-6
LOG2E = math.log2(math.e)
CONV_K = 31
FOX_HEADS = 8
FOX_HEAD_DIM = 64
MEM_HEADS = 4

LANES = 128
BF16_SUBLANES = 16
CUM_BLOCK = 128
HALO = 32
VMEM_LIMIT = 56 * 1024 * 1024


def _params(semantics):
    return pltpu.CompilerParams(dimension_semantics=semantics, vmem_limit_bytes=VMEM_LIMIT)


def _rms(x, g):
    return x * lax.rsqrt(jnp.mean(x * x, axis=-1, keepdims=True) + EPS) * g


def _split3(x):
    hi = x.astype(BF16)
    r1 = x - hi.astype(F32)
    mid = r1.astype(BF16)
    lo = (r1 - mid.astype(F32)).astype(BF16)
    return hi, mid, lo


def _dot(a, b):
    return jnp.dot(a, b, preferred_element_type=F32)


def _dot_nt(a, b):
    return lax.dot_general(a, b, (((1,), (1,)), ((), ())), preferred_element_type=F32)


def _run_staggered(chains):
    chains = list(chains)
    live = []
    while chains or live:
        if chains:
            live.append(chains.pop(0))
        for c in list(live):
            if next(c, StopIteration) is StopIteration:
                live.remove(c)


def _cast_weights_once(grid_rank, *pairs):
    first = pl.program_id(0) == 0
    for axis in range(1, grid_rank):
        first = first & (pl.program_id(axis) == 0)

    @pl.when(first)
    def _():
        for src_ref, dst_ref in pairs:
            dst_ref[...] = src_ref[...].astype(BF16)


def _mem_kv_kernel(mem_ref, g_ref, w32_ref, k_ref, v_ref, w_ref):
    _cast_weights_once(1, (w32_ref, w_ref))
    d = mem_ref.shape[-1]
    mn = _rms(mem_ref[...], g_ref[...]).astype(BF16)
    k_ref[...] = _dot(mn, w_ref[:, :d]).astype(BF16)
    v_ref[...] = _dot(mn, w_ref[:, d:]).astype(BF16)


def _mem_kv(mem, g_mem, w_mkv):
    b, m, d = mem.shape
    return pl.pallas_call(
        _mem_kv_kernel,
        grid=(b,),
        in_specs=[
            pl.BlockSpec((None, m, d), lambda i: (i, 0, 0)),
            pl.BlockSpec((1, d), lambda i: (0, 0)),
            pl.BlockSpec((None, d, 2 * d), lambda i: (0, 0, 0)),
        ],
        out_specs=[
            pl.BlockSpec((None, m, d), lambda i: (i, 0, 0)),
            pl.BlockSpec((None, m, d), lambda i: (i, 0, 0)),
        ],
        out_shape=[jax.ShapeDtypeStruct((b, m, d), BF16)] * 2,
        scratch_shapes=[pltpu.VMEM((d, 2 * d), BF16)],
        compiler_params=_params(("arbitrary",)),
        name="mem_kv",
    )(mem, g_mem, w_mkv)


def _conv_taps(win_ref, sh_ref, w_ref, o_ref, *, rows):
    tc = o_ref.shape[0]

    for r in range(1, 8):
        sh_ref[r, 8:HALO + tc, :] = win_ref[8 - r:HALO + tc - r, :]

    def shifted(r, start):
        src = win_ref[start:start + rows, :] if r == 0 else sh_ref[r, start:start + rows, :]
        return src.reshape(rows // 8, 8, src.shape[-1])

    for c in range(tc // rows):
        base = HALO + c * rows
        acc = None
        for lag in range(CONV_K):
            m, r = divmod(lag, 8)
            term = shifted(r, base - 8 * m) * w_ref[CONV_K - 1 - lag]
            acc = term if acc is None else acc + term
        o_ref[c * rows:(c + 1) * rows, :] = acc.reshape(rows, acc.shape[-1])


def _conv_norm(y, conv_b, ln_g, ln_b):
    y = y + conv_b
    mu = jnp.mean(y, axis=-1, keepdims=True)
    yc = y - mu
    yn = yc * lax.rsqrt(jnp.mean(yc * yc, axis=-1, keepdims=True) + EPS)
    yn = yn * ln_g + ln_b
    return (yn * jax.nn.sigmoid(yn)).astype(BF16)


def _in_proj_kernel(x_ref, g_ref, wt32_ref, wvt32_ref, wft32_ref, bf_ref, taps_ref,
                    conv_ref, q_ref, k_ref, vt_ref, cq_ref, ck_ref,
                    carry_ref, win_ref, sh_ref, wt_ref, wvt_ref, wft_ref, *, cw, fw, conv_rows):
    tm, d = x_ref.shape
    _cast_weights_once(2, (wt32_ref, wt_ref), (wvt32_ref, wvt_ref))

    @pl.when((pl.program_id(0) == 0) & (pl.program_id(1) == 0))
    def _():
        rep = jnp.broadcast_to(wft32_ref[...][:, None, :], (FOX_HEADS, 8, d)).reshape(FOX_HEADS * 8, d)
        wft_ref[0:FOX_HEADS * 8, :] = rep.astype(BF16)
        wft_ref[FOX_HEADS * 8:, :] = jnp.zeros((LANES - FOX_HEADS * 8, d), BF16)

    @pl.when(pl.program_id(1) == 0)
    def _():
        carry_ref[...] = jnp.zeros_like(carry_ref)
        win_ref[0:HALO, :] = jnp.zeros((HALO, win_ref.shape[1]), F32)

    @pl.when(pl.program_id(1) > 0)
    def _():
        win_ref[0:HALO, :] = win_ref[tm:tm + HALO, :]

    h = _rms(x_ref[...], g_ref[...]).astype(BF16)

    u = _dot_nt(h, wt_ref[0:cw, :])
    gate = _dot_nt(h, wt_ref[cw:2 * cw, :])
    win_ref[HALO:HALO + tm, :] = u * jax.nn.sigmoid(gate)
    _conv_taps(win_ref, sh_ref, taps_ref, conv_ref, rows=conv_rows)

    off = 2 * cw
    scale = LOG2E / math.sqrt(FOX_HEAD_DIM)
    q_ref[...] = (_dot_nt(h, wt_ref[off:off + fw, :]) * scale).astype(BF16)
    k_ref[...] = _dot_nt(h, wt_ref[off + fw:off + 2 * fw, :]).astype(BF16)
    vt_ref[...] = _dot_nt(wvt_ref[...], h).astype(BF16)

    zf = _dot_nt(h, wft_ref[...]) + bf_ref[...]
    logf = jnp.minimum(zf, 0.0) - jnp.log1p(jnp.exp(-jnp.abs(zf)))
    logf = logf * LOG2E

    row = lax.broadcasted_iota(jnp.int32, (CUM_BLOCK, CUM_BLOCK), 0)
    col = lax.broadcasted_iota(jnp.int32, (CUM_BLOCK, CUM_BLOCK), 1)
    tri = (col <= row).astype(BF16)

    lane = lax.broadcasted_iota(jnp.int32, (CUM_BLOCK, LANES), 1)
    sub = lane & 7
    valid = lane < FOX_HEADS * 8
    carry = carry_ref[...]
    for r in range(tm // CUM_BLOCK):
        blk = logf[r * CUM_BLOCK:(r + 1) * CUM_BLOCK, :]
        hi, mid, lo = _split3(blk)
        c = (_dot(tri, hi) + _dot(tri, mid)) + _dot(tri, lo) + carry
        carry = c[CUM_BLOCK - 1:CUM_BLOCK, :]
        chi, cmid, clo = (t.astype(F32) for t in _split3(c))
        one = jnp.ones_like(c)
        zero = jnp.zeros_like(c)
        cq = jnp.where(sub == 0, chi, jnp.where(sub == 1, cmid, jnp.where(sub == 2, clo,
             jnp.where(sub < 6, one, zero))))
        ck = jnp.where(sub < 3, one, jnp.where(sub == 3, -chi, jnp.where(sub == 4, -cmid,
             jnp.where(sub == 5, -clo, zero))))
        cq_ref[r * CUM_BLOCK:(r + 1) * CUM_BLOCK, :] = jnp.where(valid, cq, zero).astype(BF16)
        ck_ref[r * CUM_BLOCK:(r + 1) * CUM_BLOCK, :] = jnp.where(valid, ck, zero).astype(BF16)
    carry_ref[...] = carry


def _in_proj(x, g_mix, w_in_t, b_f, taps, *, tm, n_main, cw, fw, conv_rows):
    b, s, d = x.shape
    assert n_main % fw == 0 and (n_main + fw) % FOX_HEADS == 0 and FOX_HEADS == 8
    row = lambda i, j: (i, j, 0)
    const = lambda i, j: (0, 0)
    return pl.pallas_call(
        functools.partial(_in_proj_kernel, cw=cw, fw=fw, conv_rows=conv_rows),
        grid=(b, s // tm),
        in_specs=[
            pl.BlockSpec((None, tm, d), row),
            pl.BlockSpec((1, d), const),
            pl.BlockSpec((None, n_main, d), lambda i, j: (0, 0, 0)),
            pl.BlockSpec((None, fw, d), lambda i, j: (0, n_main // fw, 0)),
            pl.BlockSpec((None, FOX_HEADS, d), lambda i, j: (0, (n_main + fw) // FOX_HEADS, 0)),
            pl.BlockSpec((1, LANES), const),
            pl.BlockSpec((CONV_K, 8, cw), lambda i, j: (0, 0, 0)),
        ],
        out_specs=[
            pl.BlockSpec((None, tm, cw), row),
            pl.BlockSpec((None, tm, fw), row),
            pl.BlockSpec((None, tm, fw), row),
            pl.BlockSpec((None, fw, tm), lambda i, j: (i, 0, j)),
            pl.BlockSpec((None, tm, LANES), row),
            pl.BlockSpec((None, tm, LANES), row),
        ],
        out_shape=[
            jax.ShapeDtypeStruct((b, s, cw), F32),
            jax.ShapeDtypeStruct((b, s, fw), BF16),
            jax.ShapeDtypeStruct((b, s, fw), BF16),
            jax.ShapeDtypeStruct((b, fw, s), BF16),
            jax.ShapeDtypeStruct((b, s, LANES), BF16),
            jax.ShapeDtypeStruct((b, s, LANES), BF16),
        ],
        scratch_shapes=[pltpu.VMEM((1, LANES), F32), pltpu.VMEM((HALO + tm, cw), F32),
                        pltpu.VMEM((8, HALO + tm, cw), F32),
                        pltpu.VMEM((n_main, d), BF16), pltpu.VMEM((fw, d), BF16),
                        pltpu.VMEM((LANES, d), BF16)],
        compiler_params=_params(("arbitrary", "arbitrary")),
        name="in_proj",
    )(x, g_mix, w_in_t, w_in_t, w_in_t, b_f, taps)


def _fox_kernel(q_ref, cq_ref, k_ref, ck_ref, vt_ref, *rest, tb, n_weights):
    w32_refs = rest[:n_weights]
    o_ref = rest[n_weights]
    w16_refs = rest[n_weights + 1:2 * n_weights + 1]
    s_ref, p_ref = rest[2 * n_weights + 1:]
    for w32_ref, w16_ref in zip(w32_refs, w16_refs):
        w16_ref[...] = w32_ref[...].astype(BF16)

    pair = pl.program_id(1)
    s_len = q_ref.shape[0]
    lane = lax.broadcasted_iota(jnp.int32, (tb, LANES), 1)
    key = lax.broadcasted_iota(jnp.int32, (tb, tb), 0)
    qry = lax.broadcasted_iota(jnp.int32, (tb, tb), 1)
    ones = jnp.ones((16, s_len), BF16)

    def logits(qi, hh):
        lo, hi = qi * tb, (qi + 1) * tb
        buf = 2 * (qi % 2) + hh
        q2 = q_ref[lo:hi, :]
        cq = cq_ref[lo:hi, :]
        zero = jnp.zeros_like(q2)
        kcat = jnp.concatenate([k_ref[0:hi, :], ck_ref[0:hi, :]], axis=-1)
        qmask = (lane >= hh * FOX_HEAD_DIM) & (lane < (hh + 1) * FOX_HEAD_DIM)
        cmask = (lane >> 3) == pair * 2 + hh
        qcat = jnp.concatenate([jnp.where(qmask, q2, zero), jnp.where(cmask, cq, zero)], axis=-1)
        s_ref[buf, 0:hi, :] = _dot_nt(kcat, qcat)
        s_ref[buf, lo:hi, :] = jnp.where(key <= qry, s_ref[buf, lo:hi, :], -jnp.inf)

    def softmax(qi, hh):
        hi = (qi + 1) * tb
        buf = 2 * (qi % 2) + hh
        m = jnp.max(s_ref[buf, 0:hi, :], axis=0, keepdims=True)
        p_ref[buf, 0:hi, :] = jnp.exp2(s_ref[buf, 0:hi, :] - m).astype(BF16)

    heads_out = {}

    def weighted_values(qi, hh):
        lo, hi = qi * tb, (qi + 1) * tb
        buf = 2 * (qi % 2) + hh
        vaug = jnp.concatenate([vt_ref[hh * FOX_HEAD_DIM:(hh + 1) * FOX_HEAD_DIM, 0:hi],
                                ones[:, 0:hi]], axis=0)
        ot = _dot(vaug, p_ref[buf, 0:hi, :])
        heads_out[hh] = ot[0:FOX_HEAD_DIM, :] / ot[FOX_HEAD_DIM:FOX_HEAD_DIM + 1, :]
        if hh == 1:
            o_ref[lo:hi, :] = jnp.concatenate([heads_out[0], heads_out[1]], axis=0).T.astype(o_ref.dtype)

    nq = s_len // tb
    for hh in range(2):
        logits(0, hh)
    for qi in range(nq):
        for hh in range(2):
            if qi + 1 < nq:
                logits(qi + 1, hh)
            softmax(qi, hh)
            if qi >= 1:
                weighted_values(qi - 1, hh)
    for hh in range(2):
        weighted_values(nq - 1, hh)


def _row_blocks(rows, steps):
    return max(n for n in range(1, steps + 1) if rows % n == 0 and (rows // n) % BF16_SUBLANES == 0)


def _fox(q, k, vt, cq, ck, ffn_weights, *, tb):
    b, s, fw = q.shape
    pairs = fw // LANES
    qspec = pl.BlockSpec((None, s, LANES), lambda i, p: (i, 0, p))
    cspec = pl.BlockSpec((None, s, LANES), lambda i, p: (i, 0, 0))
    vspec = pl.BlockSpec((None, LANES, s), lambda i, p: (i, p, 0))
    w_views, w_specs, w_shapes = [], [], []
    for w in ffn_weights:
        rows, cols = w.shape
        n = _row_blocks(rows, b * pairs)
        w_views.append(w.reshape(n, rows // n, cols))
        w_specs.append(pl.BlockSpec((None, rows // n, cols),
                                    lambda i, p, n=n: (jnp.minimum(i * pairs + p, n - 1), 0, 0)))
        w_shapes.append(jax.ShapeDtypeStruct((n, rows // n, cols), BF16))
    att, *w_bf16 = pl.pallas_call(
        functools.partial(_fox_kernel, tb=tb, n_weights=len(ffn_weights)),
        grid=(b, pairs),
        in_specs=[qspec, cspec, qspec, cspec, vspec] + w_specs,
        out_specs=[qspec] + w_specs,
        out_shape=[jax.ShapeDtypeStruct((b, s, fw), BF16)] + w_shapes,
        scratch_shapes=[pltpu.VMEM((4, s, tb), F32), pltpu.VMEM((4, s, tb), BF16)],
        compiler_params=_params(("arbitrary", "arbitrary")),
        name="fox",
    )(q, cq, k, ck, vt, *w_views)
    return att, [wb.reshape(w.shape) for wb, w in zip(w_bf16, ffn_weights)]


def _mix_mem_kernel(x_ref, conv_ref, att_ref, cb_ref, lg_ref, lb_ref, wo32_ref, gx_ref, wq32_ref, mk_ref,
                    mv_ref, wmo32_ref, o_ref, wo_ref, wq_ref, wmo_ref, *, n_sub):
    _cast_weights_once(2, (wo32_ref, wo_ref), (wq32_ref, wq_ref), (wmo32_ref, wmo_ref))
    cw = conv_ref.shape[-1]
    tm, d = x_ref.shape
    dh = d // MEM_HEADS

    def chain(rows):
        conv = _conv_norm(conv_ref[rows, :], cb_ref[...], lg_ref[...], lb_ref[...])
        yield
        x1 = x_ref[rows, :] + _dot(conv, wo_ref[0:cw, :]) + _dot(att_ref[rows, :], wo_ref[cw:, :])
        yield
        hx = _rms(x1, gx_ref[...]).astype(BF16)
        yield
        qm = (_dot(hx, wq_ref[...]) * (1.0 / math.sqrt(dh))).astype(BF16)
        yield
        outs = []
        for h in range(MEM_HEADS):
            s = _dot_nt(qm[:, h * dh:(h + 1) * dh], mk_ref[:, h * dh:(h + 1) * dh])
            yield
            e = jnp.exp(s - jnp.max(s, axis=-1, keepdims=True))
            p = e / jnp.sum(e, axis=-1, keepdims=True)
            yield
            outs.append(_dot(p.astype(BF16), mv_ref[:, h * dh:(h + 1) * dh]).astype(BF16))
            yield
        o = jnp.concatenate(outs, axis=-1)
        o_ref[rows, :] = x1 + _dot(o, wmo_ref[...])

    sub = tm // n_sub
    _run_staggered(chain(slice(i * sub, (i + 1) * sub)) for i in range(n_sub))


def _mix_mem(x, conv_out, att_out, conv_b, ln_g, ln_b, w_out, g_x, w_mq, mk, mv, w_mo, *, tm, n_sub):
    b, s, d = x.shape
    cw = conv_out.shape[-1]
    fw = att_out.shape[-1]
    m = mk.shape[1]
    row = lambda i, j: (i, j, 0)
    const = lambda i, j: (0, 0)
    first = lambda i, j: (0, 0, 0)
    batch = lambda i, j: (i, 0, 0)
    return pl.pallas_call(
        functools.partial(_mix_mem_kernel, n_sub=n_sub),
        grid=(b, s // tm),
        in_specs=[
            pl.BlockSpec((None, tm, d), row),
            pl.BlockSpec((None, tm, cw), row),
            pl.BlockSpec((None, tm, fw), row),
            pl.BlockSpec((1, cw), const),
            pl.BlockSpec((1, cw), const),
            pl.BlockSpec((1, cw), const),
            pl.BlockSpec((None, cw + fw, d), first),
            pl.BlockSpec((1, d), const),
            pl.BlockSpec((None, d, d), first),
            pl.BlockSpec((None, m, d), batch),
            pl.BlockSpec((None, m, d), batch),
            pl.BlockSpec((None, d, d), first),
        ],
        out_specs=pl.BlockSpec((None, tm, d), row),
        out_shape=jax.ShapeDtypeStruct((b, s, d), F32),
        scratch_shapes=[pltpu.VMEM((cw + fw, d), BF16), pltpu.VMEM((d, d), BF16), pltpu.VMEM((d, d), BF16)],
        compiler_params=_params(("arbitrary", "arbitrary")),
        name="mix_mem",
    )(x, conv_out, att_out, conv_b, ln_g, ln_b, w_out, g_x, w_mq, mk, mv, w_mo)


def _ffn_kernel(x_ref, gf_ref, wgu_ref, wd_ref, gl_ref, o_ref, *, n_sub):
    tm = x_ref.shape[0]
    dff = wd_ref.shape[0]

    def chain(rows):
        x2 = x_ref[rows, :]
        hf = _rms(x2, gf_ref[...]).astype(BF16)
        yield
        g = _dot(hf, wgu_ref[:, 0:dff])
        yield
        u = _dot(hf, wgu_ref[:, dff:2 * dff])
        act = (g * jax.nn.sigmoid(g) * u).astype(BF16)
        yield
        x3 = x2 + _dot(act, wd_ref[...])
        yield
        o_ref[rows, :] = _rms(x3, gl_ref[...])

    sub = tm // n_sub
    _run_staggered(chain(slice(i * sub, (i + 1) * sub)) for i in range(n_sub))


def _ffn(x2, g_ffn, w_gu, w_down, g_final, *, tm, n_sub):
    t, d = x2.shape
    dff = w_down.shape[0]
    assert w_gu.shape == (d, 2 * dff) and dff % LANES == 0
    row = lambda i: (i, 0)
    const = lambda i: (0, 0)
    return pl.pallas_call(
        functools.partial(_ffn_kernel, n_sub=n_sub),
        grid=(t // tm,),
        in_specs=[
            pl.BlockSpec((tm, d), row),
            pl.BlockSpec((1, d), const),
            pl.BlockSpec((d, 2 * dff), const),
            pl.BlockSpec((dff, d), const),
            pl.BlockSpec((1, d), const),
        ],
        out_specs=pl.BlockSpec((tm, d), row),
        out_shape=jax.ShapeDtypeStruct((t, d), F32),
        compiler_params=_params(("parallel",)),
        name="ffn",
    )(x2, g_ffn, w_gu, w_down, g_final)


def _tile(n, want):
    t = min(n, want)
    assert n % t == 0, (n, t)
    return t


def kernel(x, mem, g_mix, w_in, b_f, conv_w, conv_b, ln_g, ln_b, w_out, g_x, g_mem, w_mq, w_mkv,
           w_mo, g_ffn, w_gu, w_down, g_final):
    b, s, d = x.shape
    cw = conv_w.shape[-1]
    fw = FOX_HEADS * FOX_HEAD_DIM
    n_main = 2 * cw + 2 * fw
    assert w_in.shape[0] == 1, "single-layer block only"
    assert w_in.shape[-1] == n_main + fw + FOX_HEADS and conv_w.shape[1] == CONV_K
    assert fw % LANES == 0 and FOX_HEADS * 8 <= LANES

    mk, mv = _mem_kv(mem, g_mem.reshape(1, d), w_mkv)

    bf = jnp.pad(jnp.repeat(b_f[0], 8), (0, LANES - FOX_HEADS * 8)).reshape(1, LANES)

    w_in_t = jnp.swapaxes(w_in, 1, 2)
    conv_taps = jnp.broadcast_to(conv_w[0][:, None, :], (CONV_K, 8, cw))
    conv_pre, q, k, vt, cq, ck = _in_proj(x, g_mix[0].reshape(1, d), w_in_t, bf, conv_taps,
                                          tm=_tile(s, 512), n_main=n_main, cw=cw, fw=fw, conv_rows=16)
    att_out, (w_gu_bf16, w_down_bf16) = _fox(q, k, vt, cq, ck, [w_gu[0], w_down[0]], tb=_tile(s, 256))
    tm_mix = _tile(s, 1024)
    x2 = _mix_mem(x, conv_pre, att_out, conv_b[0].reshape(1, cw), ln_g[0].reshape(1, cw),
                  ln_b[0].reshape(1, cw), w_out, g_x[0].reshape(1, d), w_mq, mk, mv, w_mo,
                  tm=tm_mix, n_sub=tm_mix // 256)
    y = _ffn(x2.reshape(b * s, d), g_ffn[0].reshape(1, d), w_gu_bf16, w_down_bf16,
             g_final.reshape(1, d), tm=_tile(b * s, 1024), n_sub=4)
    return y.reshape(b, s, d)
```

```python
import functools
import math

import jax
import jax.numpy as jnp
from jax import lax
from jax.experimental import pallas as pl
from jax.experimental.pallas import tpu as pltpu

F32 = jnp.float32
BF16 = jnp.bfloat16

EPS = 1e-6
LOG2E = math.log2(math.e)
CONV_K = 31
FOX_HEADS = 8
FOX_HEAD_DIM = 64
MEM_HEADS = 4

LANES = 128
BF16_SUBLANES = 16
CUM_BLOCK = 128
HALO = 32
VMEM_LIMIT = 56 * 1024 * 1024


def _params(semantics):
    return pltpu.CompilerParams(dimension_semantics=semantics, vmem_limit_bytes=VMEM_LIMIT)


def _rms(x, g):
    return x * lax.rsqrt(jnp.mean(x * x, axis=-1, keepdims=True) + EPS) * g


def _split3(x):
    hi = x.astype(BF16)
    r1 = x - hi.astype(F32)
    mid = r1.astype(BF16)
    lo = (r1 - mid.astype(F32)).astype(BF16)
    return hi, mid, lo


def _dot(a, b):
    return jnp.dot(a, b, preferred_element_type=F32)


def _dot_nt(a, b):
    return lax.dot_general(a, b, (((1,), (1,)), ((), ())), preferred_element_type=F32)


def _run_staggered(chains):
    chains = list(chains)
    live = []
    while chains or live:
        if chains:
            live.append(chains.pop(0))
        for c in list(live):
            if next(c, StopIteration) is StopIteration:
                live.remove(c)


def _cast_weights_once(grid_rank, *pairs):
    first = pl.program_id(0) == 0
    for axis in range(1, grid_rank):
        first = first & (pl.program_id(axis) == 0)

    @pl.when(first)
    def _():
        for src_ref, dst_ref in pairs:
            dst_ref[...] = src_ref[...].astype(BF16)


def _mem_kv_kernel(mem_ref, g_ref, w32_ref, k_ref, v_ref, w_ref):
    _cast_weights_once(1, (w32_ref, w_ref))
    d = mem_ref.shape[-1]
    mn = _rms(mem_ref[...], g_ref[...]).astype(BF16)
    k_ref[...] = _dot(mn, w_ref[:, :d]).astype(BF16)
    v_ref[...] = _dot(mn, w_ref[:, d:]).astype(BF16)


def _mem_kv(mem, g_mem, w_mkv):
    b, m, d = mem.shape
    return pl.pallas_call(
        _mem_kv_kernel,
        grid=(b,),
        in_specs=[
            pl.BlockSpec((None, m, d), lambda i: (i, 0, 0)),
            pl.BlockSpec((1, d), lambda i: (0, 0)),
            pl.BlockSpec((None, d, 2 * d), lambda i: (0, 0, 0)),
        ],
        out_specs=[
            pl.BlockSpec((None, m, d), lambda i: (i, 0, 0)),
            pl.BlockSpec((None, m, d), lambda i: (i, 0, 0)),
        ],
        out_shape=[jax.ShapeDtypeStruct((b, m, d), BF16)] * 2,
        scratch_shapes=[pltpu.VMEM((d, 2 * d), BF16)],
        compiler_params=_params(("arbitrary",)),
        name="mem_kv",
    )(mem, g_mem, w_mkv)


def _conv_taps(win_ref, sh_ref, w_ref, o_ref, *, rows):
    tc = o_ref.shape[0]

    for r in range(1, 8):
        sh_ref[r, 8:HALO + tc, :] = win_ref[8 - r:HALO + tc - r, :]

    def shifted(r, start):
        src = win_ref[start:start + rows, :] if r == 0 else sh_ref[r, start:start + rows, :]
        return src.reshape(rows // 8, 8, src.shape[-1])

    for c in range(tc // rows):
        base = HALO + c * rows
        acc = None
        for lag in range(CONV_K):
            m, r = divmod(lag, 8)
            term = shifted(r, base - 8 * m) * w_ref[CONV_K - 1 - lag]
            acc = term if acc is None else acc + term
        o_ref[c * rows:(c + 1) * rows, :] = acc.reshape(rows, acc.shape[-1])


def _conv_norm(y, conv_b, ln_g, ln_b):
    y = y + conv_b
    mu = jnp.mean(y, axis=-1, keepdims=True)
    yc = y - mu
    yn = yc * lax.rsqrt(jnp.mean(yc * yc, axis=-1, keepdims=True) + EPS)
    yn = yn * ln_g + ln_b
    return (yn * jax.nn.sigmoid(yn)).astype(BF16)


def _in_proj_kernel(x_ref, g_ref, wt32_ref, wvt32_ref, wft32_ref, bf_ref, taps_ref,
                    conv_ref, q_ref, k_ref, vt_ref, cq_ref, ck_ref,
                    carry_ref, win_ref, sh_ref, wt_ref, wvt_ref, wft_ref, *, cw, fw, conv_rows):
    tm, d = x_ref.shape
    _cast_weights_once(2, (wt32_ref, wt_ref), (wvt32_ref, wvt_ref))

    @pl.when((pl.program_id(0) == 0) & (pl.program_id(1) == 0))
    def _():
        rep = jnp.broadcast_to(wft32_ref[...][:, None, :], (FOX_HEADS, 8, d)).reshape(FOX_HEADS * 8, d)
        wft_ref[0:FOX_HEADS * 8, :] = rep.astype(BF16)
        wft_ref[FOX_HEADS * 8:, :] = jnp.zeros((LANES - FOX_HEADS * 8, d), BF16)

    @pl.when(pl.program_id(1) == 0)
    def _():
        carry_ref[...] = jnp.zeros_like(carry_ref)
        win_ref[0:HALO, :] = jnp.zeros((HALO, win_ref.shape[1]), F32)

    @pl.when(pl.program_id(1) > 0)
    def _():
        win_ref[0:HALO, :] = win_ref[tm:tm + HALO, :]

    h = _rms(x_ref[...], g_ref[...]).astype(BF16)

    u = _dot_nt(h, wt_ref[0:cw, :])
    gate = _dot_nt(h, wt_ref[cw:2 * cw, :])
    win_ref[HALO:HALO + tm, :] = u * jax.nn.sigmoid(gate)
    _conv_taps(win_ref, sh_ref, taps_ref, conv_ref, rows=conv_rows)

    off = 2 * cw
    scale = LOG2E / math.sqrt(FOX_HEAD_DIM)
    q_ref[...] = (_dot_nt(h, wt_ref[off:off + fw, :]) * scale).astype(BF16)
    k_ref[...] = _dot_nt(h, wt_ref[off + fw:off + 2 * fw, :]).astype(BF16)
    vt_ref[...] = _dot_nt(wvt_ref[...], h).astype(BF16)

    zf = _dot_nt(h, wft_ref[...]) + bf_ref[...]
    logf = jnp.minimum(zf, 0.0) - jnp.log1p(jnp.exp(-jnp.abs(zf)))
    logf = logf * LOG2E

    row = lax.broadcasted_iota(jnp.int32, (CUM_BLOCK, CUM_BLOCK), 0)
    col = lax.broadcasted_iota(jnp.int32, (CUM_BLOCK, CUM_BLOCK), 1)
    tri = (col <= row).astype(BF16)

    lane = lax.broadcasted_iota(jnp.int32, (CUM_BLOCK, LANES), 1)
    sub = lane & 7
    valid = lane < FOX_HEADS * 8
    carry = carry_ref[...]
    for r in range(tm // CUM_BLOCK):
        blk = logf[r * CUM_BLOCK:(r + 1) * CUM_BLOCK, :]
        hi, mid, lo = _split3(blk)
        c = (_dot(tri, hi) + _dot(tri, mid)) + _dot(tri, lo) + carry
        carry = c[CUM_BLOCK - 1:CUM_BLOCK, :]
        chi, cmid, clo = (t.astype(F32) for t in _split3(c))
        one = jnp.ones_like(c)
        zero = jnp.zeros_like(c)
        cq = jnp.where(sub == 0, chi, jnp.where(sub == 1, cmid, jnp.where(sub == 2, clo,
             jnp.where(sub < 6, one, zero))))
        ck = jnp.where(sub < 3, one, jnp.where(sub == 3, -chi, jnp.where(sub == 4, -cmid,
             jnp.where(sub == 5, -clo, zero))))
        cq_ref[r * CUM_BLOCK:(r + 1) * CUM_BLOCK, :] = jnp.where(valid, cq, zero).astype(BF16)
        ck_ref[r * CUM_BLOCK:(r + 1) * CUM_BLOCK, :] = jnp.where(valid, ck, zero).astype(BF16)
    carry_ref[...] = carry


def _in_proj(x, g_mix, w_in_t, b_f, taps, *, tm, n_main, cw, fw, conv_rows):
    b, s, d = x.shape
    assert n_main % fw == 0 and (n_main + fw) % FOX_HEADS == 0 and FOX_HEADS == 8
    row = lambda i, j: (i, j, 0)
    const = lambda i, j: (0, 0)
    return pl.pallas_call(
        functools.partial(_in_proj_kernel, cw=cw, fw=fw, conv_rows=conv_rows),
        grid=(b, s // tm),
        in_specs=[
            pl.BlockSpec((None, tm, d), row),
            pl.BlockSpec((1, d), const),
            pl.BlockSpec((None, n_main, d), lambda i, j: (0, 0, 0)),
            pl.BlockSpec((None, fw, d), lambda i, j: (0, n_main // fw, 0)),
            pl.BlockSpec((None, FOX_HEADS, d), lambda i, j: (0, (n_main + fw) // FOX_HEADS, 0)),
            pl.BlockSpec((1, LANES), const),
            pl.BlockSpec((CONV_K, 8, cw), lambda i, j: (0, 0, 0)),
        ],
        out_specs=[
            pl.BlockSpec((None, tm, cw), row),
            pl.BlockSpec((None, tm, fw), row),
            pl.BlockSpec((None, tm, fw), row),
            pl.BlockSpec((None, fw, tm), lambda i, j: (i, 0, j)),
            pl.BlockSpec((None, tm, LANES), row),
            pl.BlockSpec((None, tm, LANES), row),
        ],
        out_shape=[
            jax.ShapeDtypeStruct((b, s, cw), F32),
            jax.ShapeDtypeStruct((b, s, fw), BF16),
            jax.ShapeDtypeStruct((b, s, fw), BF16),
            jax.ShapeDtypeStruct((b, fw, s), BF16),
            jax.ShapeDtypeStruct((b, s, LANES), BF16),
            jax.ShapeDtypeStruct((b, s, LANES), BF16),
        ],
        scratch_shapes=[pltpu.VMEM((1, LANES), F32), pltpu.VMEM((HALO + tm, cw), F32),
                        pltpu.VMEM((8, HALO + tm, cw), F32),
                        pltpu.VMEM((n_main, d), BF16), pltpu.VMEM((fw, d), BF16),
                        pltpu.VMEM((LANES, d), BF16)],
        compiler_params=_params(("arbitrary", "arbitrary")),
        name="in_proj",
    )(x, g_mix, w_in_t, w_in_t, w_in_t, b_f, taps)


def _fox_kernel(q_ref, cq_ref, k_ref, ck_ref, vt_ref, *rest, tb, n_weights):
    w32_refs = rest[:n_weights]
    o_ref = rest[n_weights]
    w16_refs = rest[n_weights + 1:2 * n_weights + 1]
    s_ref, p_ref, m_ref = rest[2 * n_weights + 1:]
    for w32_ref, w16_ref in zip(w32_refs, w16_refs):
        w16_ref[...] = w32_ref[...].astype(BF16)

    pair = pl.program_id(1)
    s_len = q_ref.shape[0]
    lane = lax.broadcasted_iota(jnp.int32, (tb, LANES), 1)
    key = lax.broadcasted_iota(jnp.int32, (tb, tb), 0)
    qry = lax.broadcasted_iota(jnp.int32, (tb, tb), 1)
    ones = jnp.ones((16, s_len), BF16)

    def logits(qi, hh):
        lo, hi = qi * tb, (qi + 1) * tb
        buf = 2 * (qi % 2) + hh
        q2 = q_ref[lo:hi, :]
        cq = cq_ref[lo:hi, :]
        zero = jnp.zeros_like(q2)
        kcat = jnp.concatenate([k_ref[0:hi, :], ck_ref[0:hi, :]], axis=-1)
        qmask = (lane >= hh * FOX_HEAD_DIM) & (lane < (hh + 1) * FOX_HEAD_DIM)
        cmask = (lane >> 3) == pair * 2 + hh
        qcat = jnp.concatenate([jnp.where(qmask, q2, zero), jnp.where(cmask, cq, zero)], axis=-1)
        st = _dot_nt(kcat, qcat)
        diag = jnp.where(key <= qry, st[lo:hi, :], -jnp.inf)
        m = jnp.max(diag, axis=0, keepdims=True)
        if lo:
            s_ref[buf, 0:lo, :] = st[0:lo, :]
            m = jnp.maximum(m, jnp.max(st[0:lo, :], axis=0, keepdims=True))
        s_ref[buf, lo:hi, :] = diag
        m_ref[buf] = m

    def softmax(qi, hh):
        hi = (qi + 1) * tb
        buf = 2 * (qi % 2) + hh
        p_ref[buf, 0:hi, :] = jnp.exp2(s_ref[buf, 0:hi, :] - m_ref[buf]).astype(BF16)

    heads_out = {}

    def weighted_values(qi, hh):
        lo, hi = qi * tb, (qi + 1) * tb
        buf = 2 * (qi % 2) + hh
        vaug = jnp.concatenate([vt_ref[hh * FOX_HEAD_DIM:(hh + 1) * FOX_HEAD_DIM, 0:hi],
                                ones[:, 0:hi]], axis=0)
        ot = _dot(vaug, p_ref[buf, 0:hi, :])
        heads_out[hh] = ot[0:FOX_HEAD_DIM, :] / ot[FOX_HEAD_DIM:FOX_HEAD_DIM + 1, :]
        if hh == 1:
            o_ref[lo:hi, :] = jnp.concatenate([heads_out[0], heads_out[1]], axis=0).T.astype(o_ref.dtype)

    nq = s_len // tb
    for hh in range(2):
        logits(0, hh)
    for qi in range(nq):
        for hh in range(2):
            if qi + 1 < nq:
                logits(qi + 1, hh)
            softmax(qi, hh)
            if qi >= 1:
                weighted_values(qi - 1, hh)
    for hh in range(2):
        weighted_values(nq - 1, hh)


def _row_blocks(rows, steps):
    return max(n for n in range(1, steps + 1) if rows % n == 0 and (rows // n) % BF16_SUBLANES == 0)


def _fox(q, k, vt, cq, ck, ffn_weights, *, tb):
    b, s, fw = q.shape
    pairs = fw // LANES
    qspec = pl.BlockSpec((None, s, LANES), lambda i, p: (i, 0, p))
    cspec = pl.BlockSpec((None, s, LANES), lambda i, p: (i, 0, 0))
    vspec = pl.BlockSpec((None, LANES, s), lambda i, p: (i, p, 0))
    w_views, w_specs, w_shapes = [], [], []
    for w in ffn_weights:
        rows, cols = w.shape
        n = _row_blocks(rows, b * pairs)
        w_views.append(w.reshape(n, rows // n, cols))
        w_specs.append(pl.BlockSpec((None, rows // n, cols),
                                    lambda i, p, n=n: (jnp.minimum(i * pairs + p, n - 1), 0, 0)))
        w_shapes.append(jax.ShapeDtypeStruct((n, rows // n, cols), BF16))
    att, *w_bf16 = pl.pallas_call(
        functools.partial(_fox_kernel, tb=tb, n_weights=len(ffn_weights)),
        grid=(b, pairs),
        in_specs=[qspec, cspec, qspec, cspec, vspec] + w_specs,
        out_specs=[qspec] + w_specs,
        out_shape=[jax.ShapeDtypeStruct((b, s, fw), BF16)] + w_shapes,
        scratch_shapes=[pltpu.VMEM((4, s, tb), F32), pltpu.VMEM((4, s, tb), BF16),
                        pltpu.VMEM((4, 1, tb), F32)],
        compiler_params=_params(("arbitrary", "arbitrary")),
        name="fox",
    )(q, cq, k, ck, vt, *w_views)
    return att, [wb.reshape(w.shape) for wb, w in zip(w_bf16, ffn_weights)]


def _mix_mem_kernel(x_ref, conv_ref, att_ref, cb_ref, lg_ref, lb_ref, wo32_ref, gx_ref, wq32_ref, mk_ref,
                    mv_ref, wmo32_ref, o_ref, wo_ref, wq_ref, wmo_ref, *, n_sub):
    _cast_weights_once(2, (wo32_ref, wo_ref), (wq32_ref, wq_ref), (wmo32_ref, wmo_ref))
    cw = conv_ref.shape[-1]
    tm, d = x_ref.shape
    dh = d // MEM_HEADS

    def chain(rows):
        conv = _conv_norm(conv_ref[rows, :], cb_ref[...], lg_ref[...], lb_ref[...])
        yield
        x1 = x_ref[rows, :] + _dot(conv, wo_ref[0:cw, :]) + _dot(att_ref[rows, :], wo_ref[cw:, :])
        yield
        hx = _rms(x1, gx_ref[...]).astype(BF16)
        yield
        qm = (_dot(hx, wq_ref[...]) * (1.0 / math.sqrt(dh))).astype(BF16)
        yield
        outs = []
        for h in range(MEM_HEADS):
            s = _dot_nt(qm[:, h * dh:(h + 1) * dh], mk_ref[:, h * dh:(h + 1) * dh])
            yield
            e = jnp.exp(s - jnp.max(s, axis=-1, keepdims=True))
            p = e / jnp.sum(e, axis=-1, keepdims=True)
            yield
            outs.append(_dot(p.astype(BF16), mv_ref[:, h * dh:(h + 1) * dh]).astype(BF16))
            yield
        o = jnp.concatenate(outs, axis=-1)
        o_ref[rows, :] = x1 + _dot(o, wmo_ref[...])

    sub = tm // n_sub
    _run_staggered(chain(slice(i * sub, (i + 1) * sub)) for i in range(n_sub))


def _mix_mem(x, conv_out, att_out, conv_b, ln_g, ln_b, w_out, g_x, w_mq, mk, mv, w_mo, *, tm, n_sub):
    b, s, d = x.shape
    cw = conv_out.shape[-1]
    fw = att_out.shape[-1]
    m = mk.shape[1]
    row = lambda i, j: (i, j, 0)
    const = lambda i, j: (0, 0)
    first = lambda i, j: (0, 0, 0)
    batch = lambda i, j: (i, 0, 0)
    return pl.pallas_call(
        functools.partial(_mix_mem_kernel, n_sub=n_sub),
        grid=(b, s // tm),
        in_specs=[
            pl.BlockSpec((None, tm, d), row),
            pl.BlockSpec((None, tm, cw), row),
            pl.BlockSpec((None, tm, fw), row),
            pl.BlockSpec((1, cw), const),
            pl.BlockSpec((1, cw), const),
            pl.BlockSpec((1, cw), const),
            pl.BlockSpec((None, cw + fw, d), first),
            pl.BlockSpec((1, d), const),
            pl.BlockSpec((None, d, d), first),
            pl.BlockSpec((None, m, d), batch),
            pl.BlockSpec((None, m, d), batch),
            pl.BlockSpec((None, d, d), first),
        ],
        out_specs=pl.BlockSpec((None, tm, d), row),
        out_shape=jax.ShapeDtypeStruct((b, s, d), F32),
        scratch_shapes=[pltpu.VMEM((cw + fw, d), BF16), pltpu.VMEM((d, d), BF16), pltpu.VMEM((d, d), BF16)],
        compiler_params=_params(("arbitrary", "arbitrary")),
        name="mix_mem",
    )(x, conv_out, att_out, conv_b, ln_g, ln_b, w_out, g_x, w_mq, mk, mv, w_mo)


def _ffn_kernel(x_ref, gf_ref, wgu_ref, wd_ref, gl_ref, o_ref, *, n_sub):
    tm = x_ref.shape[0]
    dff = wd_ref.shape[0]

    def chain(rows):
        x2 = x_ref[rows, :]
        hf = _rms(x2, gf_ref[...]).astype(BF16)
        yield
        g = _dot(hf, wgu_ref[:, 0:dff])
        yield
        u = _dot(hf, wgu_ref[:, dff:2 * dff])
        act = (g * jax.nn.sigmoid(g) * u).astype(BF16)
        yield
        x3 = x2 + _dot(act, wd_ref[...])
        yield
        o_ref[rows, :] = _rms(x3, gl_ref[...])

    sub = tm // n_sub
    _run_staggered(chain(slice(i * sub, (i + 1) * sub)) for i in range(n_sub))


def _ffn(x2, g_ffn, w_gu, w_down, g_final, *, tm, n_sub):
    t, d = x2.shape
    dff = w_down.shape[0]
    assert w_gu.shape == (d, 2 * dff) and dff % LANES == 0
    row = lambda i: (i, 0)
    const = lambda i: (0, 0)
    return pl.pallas_call(
        functools.partial(_ffn_kernel, n_sub=n_sub),
        grid=(t // tm,),
        in_specs=[
            pl.BlockSpec((tm, d), row),
            pl.BlockSpec((1, d), const),
            pl.BlockSpec((d, 2 * dff), const),
            pl.BlockSpec((dff, d), const),
            pl.BlockSpec((1, d), const),
        ],
        out_specs=pl.BlockSpec((tm, d), row),
        out_shape=jax.ShapeDtypeStruct((t, d), F32),
        compiler_params=_params(("parallel",)),
        name="ffn",
    )(x2, g_ffn, w_gu, w_down, g_final)


def _tile(n, want):
    t = min(n, want)
    assert n % t == 0, (n, t)
    return t


def kernel(x, mem, g_mix, w_in, b_f, conv_w, conv_b, ln_g, ln_b, w_out, g_x, g_mem, w_mq, w_mkv,
           w_mo, g_ffn, w_gu, w_down, g_final):
    b, s, d = x.shape
    cw = conv_w.shape[-1]
    fw = FOX_HEADS * FOX_HEAD_DIM
    n_main = 2 * cw + 2 * fw
    assert w_in.shape[0] == 1, "single-layer block only"
    assert w_in.shape[-1] == n_main + fw + FOX_HEADS and conv_w.shape[1] == CONV_K
    assert fw % LANES == 0 and FOX_HEADS * 8 <= LANES

    mk, mv = _mem_kv(mem, g_mem.reshape(1, d), w_mkv)

    bf = jnp.pad(jnp.repeat(b_f[0], 8), (0, LANES - FOX_HEADS * 8)).reshape(1, LANES)

    w_in_t = jnp.swapaxes(w_in, 1, 2)
    conv_taps = jnp.broadcast_to(conv_w[0][:, None, :], (CONV_K, 8, cw))
    conv_pre, q, k, vt, cq, ck = _in_proj(x, g_mix[0].reshape(1, d), w_in_t, bf, conv_taps,
                                          tm=_tile(s, 512), n_main=n_main, cw=cw, fw=fw, conv_rows=16)
    att_out, (w_gu_bf16, w_down_bf16) = _fox(q, k, vt, cq, ck, [w_gu[0], w_down[0]], tb=_tile(s, 256))
    tm_mix = _tile(s, 1024)
    x2 = _mix_mem(x, conv_pre, att_out, conv_b[0].reshape(1, cw), ln_g[0].reshape(1, cw),
                  ln_b[0].reshape(1, cw), w_out, g_x[0].reshape(1, d), w_mq, mk, mv, w_mo,
                  tm=tm_mix, n_sub=tm_mix // 256)
    y = _ffn(x2.reshape(b * s, d), g_ffn[0].reshape(1, d), w_gu_bf16, w_down_bf16,
             g_final.reshape(1, d), tm=_tile(b * s, 1024), n_sub=4)
    return y.reshape(b, s, d)
```

```python
import functools
import math

import jax
import jax.numpy as jnp
from jax import lax
from jax.experimental import pallas as pl
from jax.experimental.pallas import tpu as pltpu

F32 = jnp.float32
BF16 = jnp.bfloat16

EPS = 1e-6
LOG2E = math.log2(math.e)
CONV_K = 31
FOX_HEADS = 8
FOX_HEAD_DIM = 64
MEM_HEADS = 4

LANES = 128
BF16_SUBLANES = 16
CUM_BLOCK = 128
HALO = 32
VMEM_LIMIT = 56 * 1024 * 1024


def _params(semantics):
    return pltpu.CompilerParams(dimension_semantics=semantics, vmem_limit_bytes=VMEM_LIMIT)


def _rms(x, g):
    return x * lax.rsqrt(jnp.mean(x * x, axis=-1, keepdims=True) + EPS) * g


def _split3(x):
    hi = x.astype(BF16)
    r1 = x - hi.astype(F32)
    mid = r1.astype(BF16)
    lo = (r1 - mid.astype(F32)).astype(BF16)
    return hi, mid, lo


def _dot(a, b):
    return jnp.dot(a, b, preferred_element_type=F32)


def _dot_nt(a, b):
    return lax.dot_general(a, b, (((1,), (1,)), ((), ())), preferred_element_type=F32)


def _run_staggered(chains):
    chains = list(chains)
    live = []
    while chains or live:
        if chains:
            live.append(chains.pop(0))
        for c in list(live):
            if next(c, StopIteration) is StopIteration:
                live.remove(c)


def _cast_weights_once(grid_rank, *pairs):
    first = pl.program_id(0) == 0
    for axis in range(1, grid_rank):
        first = first & (pl.program_id(axis) == 0)

    @pl.when(first)
    def _():
        for src_ref, dst_ref in pairs:
            dst_ref[...] = src_ref[...].astype(BF16)


def _mem_kv_kernel(mem_ref, g_ref, w32_ref, k_ref, v_ref, w_ref):
    _cast_weights_once(1, (w32_ref, w_ref))
    d = mem_ref.shape[-1]
    mn = _rms(mem_ref[...], g_ref[...]).astype(BF16)
    k_ref[...] = _dot(mn, w_ref[:, :d]).astype(BF16)
    v_ref[...] = _dot(mn, w_ref[:, d:]).astype(BF16)


def _mem_kv(mem, g_mem, w_mkv):
    b, m, d = mem.shape
    return pl.pallas_call(
        _mem_kv_kernel,
        grid=(b,),
        in_specs=[
            pl.BlockSpec((None, m, d), lambda i: (i, 0, 0)),
            pl.BlockSpec((1, d), lambda i: (0, 0)),
            pl.BlockSpec((None, d, 2 * d), lambda i: (0, 0, 0)),
        ],
        out_specs=[
            pl.BlockSpec((None, m, d), lambda i: (i, 0, 0)),
            pl.BlockSpec((None, m, d), lambda i: (i, 0, 0)),
        ],
        out_shape=[jax.ShapeDtypeStruct((b, m, d), BF16)] * 2,
        scratch_shapes=[pltpu.VMEM((d, 2 * d), BF16)],
        compiler_params=_params(("arbitrary",)),
        name="mem_kv",
    )(mem, g_mem, w_mkv)


def _conv_taps(win_ref, sh_ref, w_ref, o_ref, *, rows):
    tc = o_ref.shape[0]

    for r in range(1, 8):
        sh_ref[r, 8:HALO + tc, :] = win_ref[8 - r:HALO + tc - r, :]

    def shifted(r, start):
        src = win_ref[start:start + rows, :] if r == 0 else sh_ref[r, start:start + rows, :]
        return src.reshape(rows // 8, 8, src.shape[-1])

    for c in range(tc // rows):
        base = HALO + c * rows
        acc = None
        for lag in range(CONV_K):
            m, r = divmod(lag, 8)
            term = shifted(r, base - 8 * m) * w_ref[CONV_K - 1 - lag]
            acc = term if acc is None else acc + term
        o_ref[c * rows:(c + 1) * rows, :] = acc.reshape(rows, acc.shape[-1])


def _conv_norm(y, conv_b, ln_g, ln_b):
    y = y + conv_b
    mu = jnp.mean(y, axis=-1, keepdims=True)
    yc = y - mu
    yn = yc * lax.rsqrt(jnp.mean(yc * yc, axis=-1, keepdims=True) + EPS)
    yn = yn * ln_g + ln_b
    return (yn * jax.nn.sigmoid(yn)).astype(BF16)


def _in_proj_kernel(x_ref, g_ref, wt32_ref, wvt32_ref, wft32_ref, bf_ref, taps_ref,
                    conv_ref, q_ref, k_ref, vt_ref, cq_ref, ck_ref,
                    win_ref, sh_ref, wt_ref, wvt_ref, wft_ref, carry_ref, *, cw, fw, conv_rows):
    tm, d = x_ref.shape
    _cast_weights_once(2, (wt32_ref, wt_ref), (wvt32_ref, wvt_ref))

    @pl.when((pl.program_id(0) == 0) & (pl.program_id(1) == 0))
    def _():
        rep = jnp.broadcast_to(wft32_ref[...][:, None, :], (FOX_HEADS, 8, d)).reshape(FOX_HEADS * 8, d)
        wft_ref[0:FOX_HEADS * 8, :] = rep.astype(BF16)
        wft_ref[FOX_HEADS * 8:, :] = jnp.zeros((LANES - FOX_HEADS * 8, d), BF16)

    @pl.when(pl.program_id(1) == 0)
    def _():
        carry_ref[...] = jnp.zeros_like(carry_ref)
        win_ref[0:HALO, :] = jnp.zeros((HALO, win_ref.shape[1]), F32)

    @pl.when(pl.program_id(1) > 0)
    def _():
        win_ref[0:HALO, :] = win_ref[tm:tm + HALO, :]

    h = _rms(x_ref[...], g_ref[...]).astype(BF16)

    u = _dot_nt(h, wt_ref[0:cw, :])
    gate = _dot_nt(h, wt_ref[cw:2 * cw, :])
    win_ref[HALO:HALO + tm, :] = u * jax.nn.sigmoid(gate)
    _conv_taps(win_ref, sh_ref, taps_ref, conv_ref, rows=conv_rows)

    off = 2 * cw
    scale = LOG2E / math.sqrt(FOX_HEAD_DIM)
    q_ref[...] = (_dot_nt(h, wt_ref[off:off + fw, :]) * scale).astype(BF16)
    k_ref[...] = _dot_nt(h, wt_ref[off + fw:off + 2 * fw, :]).astype(BF16)
    vt_ref[...] = _dot_nt(wvt_ref[...], h).astype(BF16)

    zf = _dot_nt(h, wft_ref[...]) + bf_ref[...]
    logf = jnp.minimum(zf, 0.0) - jnp.log1p(jnp.exp(-jnp.abs(zf)))
    logf = logf * LOG2E

    row = lax.broadcasted_iota(jnp.int32, (CUM_BLOCK, CUM_BLOCK), 0)
    col = lax.broadcasted_iota(jnp.int32, (CUM_BLOCK, CUM_BLOCK), 1)
    tri = (col <= row).astype(BF16)

    lane = lax.broadcasted_iota(jnp.int32, (CUM_BLOCK, LANES), 1)
    sub = lane & 7
    valid = lane < FOX_HEADS * 8
    carry = carry_ref[...]
    for r in range(tm // CUM_BLOCK):
        blk = logf[r * CUM_BLOCK:(r + 1) * CUM_BLOCK, :]
        hi, mid, lo = _split3(blk)
        c = (_dot(tri, hi) + _dot(tri, mid)) + _dot(tri, lo) + carry
        carry = c[CUM_BLOCK - 1:CUM_BLOCK, :]
        chi, cmid, clo = (t.astype(F32) for t in _split3(c))
        one = jnp.ones_like(c)
        zero = jnp.zeros_like(c)
        cq = jnp.where(sub == 0, chi, jnp.where(sub == 1, cmid, jnp.where(sub == 2, clo,
             jnp.where(sub < 6, one, zero))))
        ck = jnp.where(sub < 3, one, jnp.where(sub == 3, -chi, jnp.where(sub == 4, -cmid,
             jnp.where(sub == 5, -clo, zero))))
        cq_ref[r * CUM_BLOCK:(r + 1) * CUM_BLOCK, :] = jnp.where(valid, cq, zero).astype(BF16)
        ck_ref[r * CUM_BLOCK:(r + 1) * CUM_BLOCK, :] = jnp.where(valid, ck, zero).astype(BF16)
    carry_ref[...] = carry


def _in_proj(x, g_mix, w_in_t, b_f, taps, *, tm, n_main, cw, fw, conv_rows):
    b, s, d = x.shape
    assert n_main % fw == 0 and (n_main + fw) % FOX_HEADS == 0 and FOX_HEADS == 8
    row = lambda i, j: (i, j, 0)
    const = lambda i, j: (0, 0)
    return pl.pallas_call(
        functools.partial(_in_proj_kernel, cw=cw, fw=fw, conv_rows=conv_rows),
        grid=(b, s // tm),
        in_specs=[
            pl.BlockSpec((None, tm, d), row),
            pl.BlockSpec((1, d), const),
            pl.BlockSpec((None, n_main, d), lambda i, j: (0, 0, 0)),
            pl.BlockSpec((None, fw, d), lambda i, j: (0, n_main // fw, 0)),
            pl.BlockSpec((None, FOX_HEADS, d), lambda i, j: (0, (n_main + fw) // FOX_HEADS, 0)),
            pl.BlockSpec((1, LANES), const),
            pl.BlockSpec((CONV_K, 8, cw), lambda i, j: (0, 0, 0)),
        ],
        out_specs=[
            pl.BlockSpec((None, tm, cw), row),
            pl.BlockSpec((None, tm, fw), row),
            pl.BlockSpec((None, tm, fw), row),
            pl.BlockSpec((None, fw, tm), lambda i, j: (i, 0, j)),
            pl.BlockSpec((None, tm, LANES), row),
            pl.BlockSpec((None, tm, LANES), row),
        ],
        out_shape=[
            jax.ShapeDtypeStruct((b, s, cw), F32),
            jax.ShapeDtypeStruct((b, s, fw), BF16),
            jax.ShapeDtypeStruct((b, s, fw), BF16),
            jax.ShapeDtypeStruct((b, fw, s), BF16),
            jax.ShapeDtypeStruct((b, s, LANES), BF16),
            jax.ShapeDtypeStruct((b, s, LANES), BF16),
        ],
        scratch_shapes=[pltpu.VMEM((HALO + tm, cw), F32), pltpu.VMEM((8, HALO + tm, cw), F32),
                        pltpu.VMEM((n_main, d), BF16), pltpu.VMEM((fw, d), BF16),
                        pltpu.VMEM((LANES, d), BF16), pltpu.VMEM((1, LANES), F32)],
        compiler_params=_params(("arbitrary", "arbitrary")),
        name="in_proj",
    )(x, g_mix, w_in_t, w_in_t, w_in_t, b_f, taps)


def _fox_kernel(q_ref, cq_ref, k_ref, ck_ref, vt_ref, *rest, tb, n_weights):
    w32_refs = rest[:n_weights]
    o_ref = rest[n_weights]
    w16_refs = rest[n_weights + 1:2 * n_weights + 1]
    s_ref, p_ref = rest[2 * n_weights + 1:]
    for w32_ref, w16_ref in zip(w32_refs, w16_refs):
        w16_ref[...] = w32_ref[...].astype(BF16)

    pair = pl.program_id(1)
    s_len = q_ref.shape[0]
    lane = lax.broadcasted_iota(jnp.int32, (tb, LANES), 1)
    key = lax.broadcasted_iota(jnp.int32, (tb, tb), 0)
    qry = lax.broadcasted_iota(jnp.int32, (tb, tb), 1)
    ones = jnp.ones((16, s_len), BF16)

    def logits(qi, hh):
        lo, hi = qi * tb, (qi + 1) * tb
        buf = 2 * (qi % 2) + hh
        q2 = q_ref[lo:hi, :]
        cq = cq_ref[lo:hi, :]
        zero = jnp.zeros_like(q2)
        kcat = jnp.concatenate([k_ref[0:hi, :], ck_ref[0:hi, :]], axis=-1)
        qmask = (lane >= hh * FOX_HEAD_DIM) & (lane < (hh + 1) * FOX_HEAD_DIM)
        cmask = (lane >> 3) == pair * 2 + hh
        qcat = jnp.concatenate([jnp.where(qmask, q2, zero), jnp.where(cmask, cq, zero)], axis=-1)
        s_ref[buf, 0:hi, :] = _dot_nt(kcat, qcat)
        s_ref[buf, lo:hi, :] = jnp.where(key <= qry, s_ref[buf, lo:hi, :], -jnp.inf)

    def softmax(qi, hh):
        hi = (qi + 1) * tb
        buf = 2 * (qi % 2) + hh
        m = jnp.max(s_ref[buf, 0:hi, :], axis=0, keepdims=True)
        p_ref[buf, 0:hi, :] = jnp.exp2(s_ref[buf, 0:hi, :] - m).astype(BF16)

    heads_out = {}

    def weighted_values(qi, hh):
        lo, hi = qi * tb, (qi + 1) * tb
        buf = 2 * (qi % 2) + hh
        vaug = jnp.concatenate([vt_ref[hh * FOX_HEAD_DIM:(hh + 1) * FOX_HEAD_DIM, 0:hi],
                                ones[:, 0:hi]], axis=0)
        ot = _dot(vaug, p_ref[buf, 0:hi, :])
        heads_out[hh] = ot[0:FOX_HEAD_DIM, :] / ot[FOX_HEAD_DIM:FOX_HEAD_DIM + 1, :]
        if hh == 1:
            o_ref[lo:hi, :] = jnp.concatenate([heads_out[0], heads_out[1]], axis=0).T.astype(o_ref.dtype)

    nq = s_len // tb
    for hh in range(2):
        logits(0, hh)
    for qi in range(nq):
        for hh in range(2):
            if qi + 1 < nq:
                logits(qi + 1, hh)
            softmax(qi, hh)
            if qi >= 1:
                weighted_values(qi - 1, hh)
    for hh in range(2):
        weighted_values(nq - 1, hh)


def _row_blocks(rows, steps):
    return max(n for n in range(1, steps + 1) if rows % n == 0 and (rows // n) % BF16_SUBLANES == 0)


def _fox(q, k, vt, cq, ck, ffn_weights, *, tb):
    b, s, fw = q.shape
    pairs = fw // LANES
    qspec = pl.BlockSpec((None, s, LANES), lambda i, p: (i, 0, p))
    cspec = pl.BlockSpec((None, s, LANES), lambda i, p: (i, 0, 0))
    vspec = pl.BlockSpec((None, LANES, s), lambda i, p: (i, p, 0))
    w_views, w_specs, w_shapes = [], [], []
    for w in ffn_weights:
        rows, cols = w.shape
        n = _row_blocks(rows, b * pairs)
        w_views.append(w.reshape(n, rows // n, cols))
        w_specs.append(pl.BlockSpec((None, rows // n, cols),
                                    lambda i, p, n=n: (jnp.minimum(i * pairs + p, n - 1), 0, 0)))
        w_shapes.append(jax.ShapeDtypeStruct((n, rows // n, cols), BF16))
    att, *w_bf16 = pl.pallas_call(
        functools.partial(_fox_kernel, tb=tb, n_weights=len(ffn_weights)),
        grid=(b, pairs),
        in_specs=[qspec, cspec, qspec, cspec, vspec] + w_specs,
        out_specs=[qspec] + w_specs,
        out_shape=[jax.ShapeDtypeStruct((b, s, fw), BF16)] + w_shapes,
        scratch_shapes=[pltpu.VMEM((4, s, tb), F32), pltpu.VMEM((4, s, tb), BF16)],
        compiler_params=_params(("arbitrary", "arbitrary")),
        name="fox",
    )(q, cq, k, ck, vt, *w_views)
    return att, [wb.reshape(w.shape) for wb, w in zip(w_bf16, ffn_weights)]


def _mix_mem_kernel(x_ref, conv_ref, att_ref, cb_ref, lg_ref, lb_ref, wo32_ref, gx_ref, wq32_ref, mk_ref,
                    mv_ref, wmo32_ref, o_ref, wo_ref, wq_ref, wmo_ref, *, n_sub):
    _cast_weights_once(2, (wo32_ref, wo_ref), (wq32_ref, wq_ref), (wmo32_ref, wmo_ref))
    cw = conv_ref.shape[-1]
    tm, d = x_ref.shape
    dh = d // MEM_HEADS

    def chain(rows):
        x1 = x_ref[rows, :] + _dot(att_ref[rows, :], wo_ref[cw:, :])
        conv = _conv_norm(conv_ref[rows, :], cb_ref[...], lg_ref[...], lb_ref[...])
        yield
        x1 = x1 + _dot(conv, wo_ref[0:cw, :])
        yield
        hx = _rms(x1, gx_ref[...]).astype(BF16)
        yield
        qm = (_dot(hx, wq_ref[...]) * (1.0 / math.sqrt(dh))).astype(BF16)
        yield
        outs = []
        for h in range(MEM_HEADS):
            s = _dot_nt(qm[:, h * dh:(h + 1) * dh], mk_ref[:, h * dh:(h + 1) * dh])
            yield
            e = jnp.exp(s - jnp.max(s, axis=-1, keepdims=True))
            p = e / jnp.sum(e, axis=-1, keepdims=True)
            yield
            outs.append(_dot(p.astype(BF16), mv_ref[:, h * dh:(h + 1) * dh]).astype(BF16))
            yield
        o = jnp.concatenate(outs, axis=-1)
        o_ref[rows, :] = x1 + _dot(o, wmo_ref[...])

    sub = tm // n_sub
    _run_staggered(chain(slice(i * sub, (i + 1) * sub)) for i in range(n_sub))


def _mix_mem(x, conv_out, att_out, conv_b, ln_g, ln_b, w_out, g_x, w_mq, mk, mv, w_mo, *, tm, n_sub):
    b, s, d = x.shape
    cw = conv_out.shape[-1]
    fw = att_out.shape[-1]
    m = mk.shape[1]
    row = lambda i, j: (i, j, 0)
    const = lambda i, j: (0, 0)
    first = lambda i, j: (0, 0, 0)
    batch = lambda i, j: (i, 0, 0)
    return pl.pallas_call(
        functools.partial(_mix_mem_kernel, n_sub=n_sub),
        grid=(b, s // tm),
        in_specs=[
            pl.BlockSpec((None, tm, d), row),
            pl.BlockSpec((None, tm, cw), row),
            pl.BlockSpec((None, tm, fw), row),
            pl.BlockSpec((1, cw), const),
            pl.BlockSpec((1, cw), const),
            pl.BlockSpec((1, cw), const),
            pl.BlockSpec((None, cw + fw, d), first),
            pl.BlockSpec((1, d), const),
            pl.BlockSpec((None, d, d), first),
            pl.BlockSpec((None, m, d), batch),
            pl.BlockSpec((None, m, d), batch),
            pl.BlockSpec((None, d, d), first),
        ],
        out_specs=pl.BlockSpec((None, tm, d), row),
        out_shape=jax.ShapeDtypeStruct((b, s, d), F32),
        scratch_shapes=[pltpu.VMEM((cw + fw, d), BF16), pltpu.VMEM((d, d), BF16), pltpu.VMEM((d, d), BF16)],
        compiler_params=_params(("arbitrary", "arbitrary")),
        name="mix_mem",
    )(x, conv_out, att_out, conv_b, ln_g, ln_b, w_out, g_x, w_mq, mk, mv, w_mo)


def _ffn_kernel(x_ref, gf_ref, wgu_ref, wd_ref, gl_ref, o_ref, *, n_sub):
    tm = x_ref.shape[0]
    dff = wd_ref.shape[0]

    def chain(rows):
        x2 = x_ref[rows, :]
        hf = _rms(x2, gf_ref[...]).astype(BF16)
        yield
        g = _dot(hf, wgu_ref[:, 0:dff])
        yield
        u = _dot(hf, wgu_ref[:, dff:2 * dff])
        act = (g * jax.nn.sigmoid(g) * u).astype(BF16)
        yield
        x3 = x2 + _dot(act, wd_ref[...])
        yield
        o_ref[rows, :] = _rms(x3, gl_ref[...])

    sub = tm // n_sub
    _run_staggered(chain(slice(i * sub, (i + 1) * sub)) for i in range(n_sub))


def _ffn(x2, g_ffn, w_gu, w_down, g_final, *, tm, n_sub):
    t, d = x2.shape
    dff = w_down.shape[0]
    assert w_gu.shape == (d, 2 * dff) and dff % LANES == 0
    row = lambda i: (i, 0)
    const = lambda i: (0, 0)
    return pl.pallas_call(
        functools.partial(_ffn_kernel, n_sub=n_sub),
        grid=(t // tm,),
        in_specs=[
            pl.BlockSpec((tm, d), row),
            pl.BlockSpec((1, d), const),
            pl.BlockSpec((d, 2 * dff), const),
            pl.BlockSpec((dff, d), const),
            pl.BlockSpec((1, d), const),
        ],
        out_specs=pl.BlockSpec((tm, d), row),
        out_shape=jax.ShapeDtypeStruct((t, d), F32),
        compiler_params=_params(("parallel",)),
        name="ffn",
    )(x2, g_ffn, w_gu, w_down, g_final)


def _tile(n, want):
    t = min(n, want)
    assert n % t == 0, (n, t)
    return t


def kernel(x, mem, g_mix, w_in, b_f, conv_w, conv_b, ln_g, ln_b, w_out, g_x, g_mem, w_mq, w_mkv,
           w_mo, g_ffn, w_gu, w_down, g_final):
    b, s, d = x.shape
    cw = conv_w.shape[-1]
    fw = FOX_HEADS * FOX_HEAD_DIM
    n_main = 2 * cw + 2 * fw
    assert w_in.shape[0] == 1, "single-layer block only"
    assert w_in.shape[-1] == n_main + fw + FOX_HEADS and conv_w.shape[1] == CONV_K
    assert fw % LANES == 0 and FOX_HEADS * 8 <= LANES

    mk, mv = _mem_kv(mem, g_mem.reshape(1, d), w_mkv)

    bf = jnp.pad(jnp.repeat(b_f[0], 8), (0, LANES - FOX_HEADS * 8)).reshape(1, LANES)

    w_in_t = jnp.swapaxes(w_in, 1, 2)
    conv_taps = jnp.broadcast_to(conv_w[0][:, None, :], (CONV_K, 8, cw))
    conv_pre, q, k, vt, cq, ck = _in_proj(x, g_mix[0].reshape(1, d), w_in_t, bf, conv_taps,
                                          tm=_tile(s, 512), n_main=n_main, cw=cw, fw=fw, conv_rows=16)
    att_out, (w_gu_bf16, w_down_bf16) = _fox(q, k, vt, cq, ck, [w_gu[0], w_down[0]], tb=_tile(s, 256))
    tm_mix = _tile(s, 1024)
    x2 = _mix_mem(x, conv_pre, att_out, conv_b[0].reshape(1, cw), ln_g[0].reshape(1, cw),
                  ln_b[0].reshape(1, cw), w_out, g_x[0].reshape(1, d), w_mq, mk, mv, w_mo,
                  tm=tm_mix, n_sub=tm_mix // 256)
    y = _ffn(x2.reshape(b * s, d), g_ffn[0].reshape(1, d), w_gu_bf16, w_down_bf16,
             g_final.reshape(1, d), tm=_tile(b * s, 1024), n_sub=4)
    return y.reshape(b, s, d)
```

```python
import functools
import math

import jax
import jax.numpy as jnp
from jax import lax
from jax.experimental import pallas as pl
from jax.experimental.pallas import tpu as pltpu

F32 = jnp.float32
BF16 = jnp.bfloat16

EPS = 1e-6
LOG2E = math.log2(math.e)
CONV_K = 31
FOX_HEADS = 8
FOX_HEAD_DIM = 64
MEM_HEADS = 4

LANES = 128
BF16_SUBLANES = 16
CUM_BLOCK = 128
HALO = 32
VMEM_LIMIT = 56 * 1024 * 1024


def _params(semantics):
    return pltpu.CompilerParams(dimension_semantics=semantics, vmem_limit_bytes=VMEM_LIMIT)


def _rms(x, g):
    return x * lax.rsqrt(jnp.mean(x * x, axis=-1, keepdims=True) + EPS) * g


def _split3(x):
    hi = x.astype(BF16)
    r1 = x - hi.astype(F32)
    mid = r1.astype(BF16)
    lo = (r1 - mid.astype(F32)).astype(BF16)
    return hi, mid, lo


def _dot(a, b):
    return jnp.dot(a, b, preferred_element_type=F32)


def _dot_nt(a, b):
    return lax.dot_general(a, b, (((1,), (1,)), ((), ())), preferred_element_type=F32)


def _run_staggered(chains):
    chains = list(chains)
    live = []
    while chains or live:
        if chains:
            live.append(chains.pop(0))
        for c in list(live):
            if next(c, StopIteration) is StopIteration:
                live.remove(c)


def _cast_weights_once(grid_rank, *pairs):
    first = pl.program_id(0) == 0
    for axis in range(1, grid_rank):
        first = first & (pl.program_id(axis) == 0)

    @pl.when(first)
    def _():
        for src_ref, dst_ref in pairs:
            dst_ref[...] = src_ref[...].astype(BF16)


def _mem_kv_kernel(mem_ref, g_ref, w32_ref, k_ref, v_ref, w_ref):
    _cast_weights_once(1, (w32_ref, w_ref))
    d = mem_ref.shape[-1]
    mn = _rms(mem_ref[...], g_ref[...]).astype(BF16)
    k_ref[...] = _dot(mn, w_ref[:, :d]).astype(BF16)
    v_ref[...] = _dot(mn, w_ref[:, d:]).astype(BF16)


def _mem_kv(mem, g_mem, w_mkv):
    b, m, d = mem.shape
    return pl.pallas_call(
        _mem_kv_kernel,
        grid=(b,),
        in_specs=[
            pl.BlockSpec((None, m, d), lambda i: (i, 0, 0)),
            pl.BlockSpec((1, d), lambda i: (0, 0)),
            pl.BlockSpec((None, d, 2 * d), lambda i: (0, 0, 0)),
        ],
        out_specs=[
            pl.BlockSpec((None, m, d), lambda i: (i, 0, 0)),
            pl.BlockSpec((None, m, d), lambda i: (i, 0, 0)),
        ],
        out_shape=[jax.ShapeDtypeStruct((b, m, d), BF16)] * 2,
        scratch_shapes=[pltpu.VMEM((d, 2 * d), BF16)],
        compiler_params=_params(("arbitrary",)),
        name="mem_kv",
    )(mem, g_mem, w_mkv)


def _conv_taps(win_ref, sh_ref, w_ref, o_ref, *, rows):
    tc = o_ref.shape[0]

    for r in range(1, 8):
        sh_ref[r, 8:HALO + tc, :] = win_ref[8 - r:HALO + tc - r, :]

    def shifted(r, start):
        src = win_ref[start:start + rows, :] if r == 0 else sh_ref[r, start:start + rows, :]
        return src.reshape(rows // 8, 8, src.shape[-1])

    for c in range(tc // rows):
        base = HALO + c * rows
        acc = None
        for lag in range(CONV_K):
            m, r = divmod(lag, 8)
            term = shifted(r, base - 8 * m) * w_ref[CONV_K - 1 - lag]
            acc = term if acc is None else acc + term
        o_ref[c * rows:(c + 1) * rows, :] = acc.reshape(rows, acc.shape[-1])


def _conv_norm(y, conv_b, ln_g, ln_b):
    y = y + conv_b
    mu = jnp.mean(y, axis=-1, keepdims=True)
    yc = y - mu
    yn = yc * lax.rsqrt(jnp.mean(yc * yc, axis=-1, keepdims=True) + EPS)
    yn = yn * ln_g + ln_b
    return (yn * jax.nn.sigmoid(yn)).astype(BF16)


def _in_proj_kernel(x_ref, g_ref, wt32_ref, wvt32_ref, wft32_ref, bf_ref, taps_ref,
                    conv_ref, q_ref, k_ref, vt_ref, cq_ref, ck_ref,
                    win_ref, sh_ref, wt_ref, wvt_ref, wft_ref, carry_ref, *, cw, fw, conv_rows):
    tm, d = x_ref.shape
    _cast_weights_once(2, (wt32_ref, wt_ref), (wvt32_ref, wvt_ref))

    @pl.when((pl.program_id(0) == 0) & (pl.program_id(1) == 0))
    def _():
        rep = jnp.broadcast_to(wft32_ref[...][:, None, :], (FOX_HEADS, 8, d)).reshape(FOX_HEADS * 8, d)
        wft_ref[0:FOX_HEADS * 8, :] = rep.astype(BF16)
        wft_ref[FOX_HEADS * 8:, :] = jnp.zeros((LANES - FOX_HEADS * 8, d), BF16)

    @pl.when(pl.program_id(1) == 0)
    def _():
        carry_ref[...] = jnp.zeros_like(carry_ref)
        win_ref[0:HALO, :] = jnp.zeros((HALO, win_ref.shape[1]), F32)

    @pl.when(pl.program_id(1) > 0)
    def _():
        win_ref[0:HALO, :] = win_ref[tm:tm + HALO, :]

    h = _rms(x_ref[...], g_ref[...]).astype(BF16)

    u = _dot_nt(h, wt_ref[0:cw, :])
    gate = _dot_nt(h, wt_ref[cw:2 * cw, :])
    win_ref[HALO:HALO + tm, :] = u * jax.nn.sigmoid(gate)
    _conv_taps(win_ref, sh_ref, taps_ref, conv_ref, rows=conv_rows)

    off = 2 * cw
    scale = LOG2E / math.sqrt(FOX_HEAD_DIM)
    q_ref[...] = (_dot_nt(h, wt_ref[off:off + fw, :]) * scale).astype(BF16)
    k_ref[...] = _dot_nt(h, wt_ref[off + fw:off + 2 * fw, :]).astype(BF16)
    vt_ref[...] = _dot_nt(wvt_ref[...], h).astype(BF16)

    zf = _dot_nt(h, wft_ref[...]) + bf_ref[...]
    logf = jnp.minimum(zf, 0.0) - jnp.log1p(jnp.exp(-jnp.abs(zf)))
    logf = logf * LOG2E

    row = lax.broadcasted_iota(jnp.int32, (CUM_BLOCK, CUM_BLOCK), 0)
    col = lax.broadcasted_iota(jnp.int32, (CUM_BLOCK, CUM_BLOCK), 1)
    tri = (col <= row).astype(BF16)

    lane = lax.broadcasted_iota(jnp.int32, (CUM_BLOCK, LANES), 1)
    sub = lane & 7
    valid = lane < FOX_HEADS * 8
    carry = carry_ref[...]
    for r in range(tm // CUM_BLOCK):
        blk = logf[r * CUM_BLOCK:(r + 1) * CUM_BLOCK, :]
        hi, mid, lo = _split3(blk)
        c = (_dot(tri, hi) + _dot(tri, mid)) + _dot(tri, lo) + carry
        carry = c[CUM_BLOCK - 1:CUM_BLOCK, :]
        chi, cmid, clo = (t.astype(F32) for t in _split3(c))
        one = jnp.ones_like(c)
        zero = jnp.zeros_like(c)
        cq = jnp.where(sub == 0, chi, jnp.where(sub == 1, cmid, jnp.where(sub == 2, clo,
             jnp.where(sub < 6, one, zero))))
        ck = jnp.where(sub < 3, one, jnp.where(sub == 3, -chi, jnp.where(sub == 4, -cmid,
             jnp.where(sub == 5, -clo, zero))))
        cq_ref[r * CUM_BLOCK:(r + 1) * CUM_BLOCK, :] = jnp.where(valid, cq, zero).astype(BF16)
        ck_ref[r * CUM_BLOCK:(r + 1) * CUM_BLOCK, :] = jnp.where(valid, ck, zero).astype(BF16)
    carry_ref[...] = carry


def _in_proj(x, g_mix, w_in_t, b_f, taps, *, tm, n_main, cw, fw, conv_rows):
    b, s, d = x.shape
    assert n_main % fw == 0 and (n_main + fw) % FOX_HEADS == 0 and FOX_HEADS == 8
    row = lambda i, j: (i, j, 0)
    const = lambda i, j: (0, 0)
    return pl.pallas_call(
        functools.partial(_in_proj_kernel, cw=cw, fw=fw, conv_rows=conv_rows),
        grid=(b, s // tm),
        in_specs=[
            pl.BlockSpec((None, tm, d), row),
            pl.BlockSpec((1, d), const),
            pl.BlockSpec((None, n_main, d), lambda i, j: (0, 0, 0)),
            pl.BlockSpec((None, fw, d), lambda i, j: (0, n_main // fw, 0)),
            pl.BlockSpec((None, FOX_HEADS, d), lambda i, j: (0, (n_main + fw) // FOX_HEADS, 0)),
            pl.BlockSpec((1, LANES), const),
            pl.BlockSpec((CONV_K, 8, cw), lambda i, j: (0, 0, 0)),
        ],
        out_specs=[
            pl.BlockSpec((None, tm, cw), row),
            pl.BlockSpec((None, tm, fw), row),
            pl.BlockSpec((None, tm, fw), row),
            pl.BlockSpec((None, fw, tm), lambda i, j: (i, 0, j)),
            pl.BlockSpec((None, tm, LANES), row),
            pl.BlockSpec((None, tm, LANES), row),
        ],
        out_shape=[
            jax.ShapeDtypeStruct((b, s, cw), F32),
            jax.ShapeDtypeStruct((b, s, fw), BF16),
            jax.ShapeDtypeStruct((b, s, fw), BF16),
            jax.ShapeDtypeStruct((b, fw, s), BF16),
            jax.ShapeDtypeStruct((b, s, LANES), BF16),
            jax.ShapeDtypeStruct((b, s, LANES), BF16),
        ],
        scratch_shapes=[pltpu.VMEM((HALO + tm, cw), F32), pltpu.VMEM((8, HALO + tm, cw), F32),
                        pltpu.VMEM((n_main, d), BF16), pltpu.VMEM((fw, d), BF16),
                        pltpu.VMEM((LANES, d), BF16), pltpu.VMEM((1, LANES), F32)],
        compiler_params=_params(("arbitrary", "arbitrary")),
        name="in_proj",
    )(x, g_mix, w_in_t, w_in_t, w_in_t, b_f, taps)


def _fox_kernel(q_ref, cq_ref, k_ref, ck_ref, vt_ref, *rest, tb, n_weights):
    w32_refs = rest[:n_weights]
    o_ref = rest[n_weights]
    w16_refs = rest[n_weights + 1:2 * n_weights + 1]
    s_ref, p_ref = rest[2 * n_weights + 1:]
    for w32_ref, w16_ref in zip(w32_refs, w16_refs):
        w16_ref[...] = w32_ref[...].astype(BF16)

    s_len = q_ref.shape[0]
    pps = q_ref.shape[1] // LANES
    first_pair = pl.program_id(1) * pps
    lane = lax.broadcasted_iota(jnp.int32, (tb, LANES), 1)
    key = lax.broadcasted_iota(jnp.int32, (tb, tb), 0)
    qry = lax.broadcasted_iota(jnp.int32, (tb, tb), 1)
    ones = jnp.ones((16, s_len), BF16)

    blocks = [(pp, qi) for pp in range(pps) for qi in range(s_len // tb)]

    def logits(n, hh):
        pp, qi = blocks[n]
        lo, hi = qi * tb, (qi + 1) * tb
        buf = 2 * (n % 2) + hh
        lanes = slice(pp * LANES, (pp + 1) * LANES)
        q2 = q_ref[lo:hi, lanes]
        cq = cq_ref[lo:hi, :]
        zero = jnp.zeros_like(q2)
        kcat = jnp.concatenate([k_ref[0:hi, lanes], ck_ref[0:hi, :]], axis=-1)
        qmask = (lane >= hh * FOX_HEAD_DIM) & (lane < (hh + 1) * FOX_HEAD_DIM)
        cmask = (lane >> 3) == (first_pair + pp) * 2 + hh
        qcat = jnp.concatenate([jnp.where(qmask, q2, zero), jnp.where(cmask, cq, zero)], axis=-1)
        s_ref[buf, 0:hi, :] = _dot_nt(kcat, qcat)
        s_ref[buf, lo:hi, :] = jnp.where(key <= qry, s_ref[buf, lo:hi, :], -jnp.inf)

    def softmax(n, hh):
        hi = (blocks[n][1] + 1) * tb
        buf = 2 * (n % 2) + hh
        m = jnp.max(s_ref[buf, 0:hi, :], axis=0, keepdims=True)
        p_ref[buf, 0:hi, :] = jnp.exp2(s_ref[buf, 0:hi, :] - m).astype(BF16)

    heads_out = {}

    def weighted_values(n, hh):
        pp, qi = blocks[n]
        lo, hi = qi * tb, (qi + 1) * tb
        buf = 2 * (n % 2) + hh
        row0 = pp * LANES + hh * FOX_HEAD_DIM
        vaug = jnp.concatenate([vt_ref[row0:row0 + FOX_HEAD_DIM, 0:hi], ones[:, 0:hi]], axis=0)
        ot = _dot(vaug, p_ref[buf, 0:hi, :])
        heads_out[hh] = ot[0:FOX_HEAD_DIM, :] / ot[FOX_HEAD_DIM:FOX_HEAD_DIM + 1, :]
        if hh == 1:
            o_ref[lo:hi, pp * LANES:(pp + 1) * LANES] = (
                jnp.concatenate([heads_out[0], heads_out[1]], axis=0).T.astype(o_ref.dtype))

    for hh in range(2):
        logits(0, hh)
    for n in range(len(blocks)):
        for hh in range(2):
            if n + 1 < len(blocks):
                logits(n + 1, hh)
            softmax(n, hh)
            if n >= 1:
                weighted_values(n - 1, hh)
    for hh in range(2):
        weighted_values(len(blocks) - 1, hh)


def _row_blocks(rows, steps):
    return max(n for n in range(1, steps + 1) if rows % n == 0 and (rows // n) % BF16_SUBLANES == 0)


def _fox(q, k, vt, cq, ck, ffn_weights, *, tb, pps):
    b, s, fw = q.shape
    steps = fw // (pps * LANES)
    qspec = pl.BlockSpec((None, s, pps * LANES), lambda i, p: (i, 0, p))
    cspec = pl.BlockSpec((None, s, LANES), lambda i, p: (i, 0, 0))
    vspec = pl.BlockSpec((None, pps * LANES, s), lambda i, p: (i, p, 0))
    w_views, w_specs, w_shapes = [], [], []
    for w in ffn_weights:
        rows, cols = w.shape
        n = _row_blocks(rows, b * steps)
        w_views.append(w.reshape(n, rows // n, cols))
        w_specs.append(pl.BlockSpec((None, rows // n, cols),
                                    lambda i, p, n=n: (jnp.minimum(i * steps + p, n - 1), 0, 0)))
        w_shapes.append(jax.ShapeDtypeStruct((n, rows // n, cols), BF16))
    att, *w_bf16 = pl.pallas_call(
        functools.partial(_fox_kernel, tb=tb, n_weights=len(ffn_weights)),
        grid=(b, steps),
        in_specs=[qspec, cspec, qspec, cspec, vspec] + w_specs,
        out_specs=[qspec] + w_specs,
        out_shape=[jax.ShapeDtypeStruct((b, s, fw), BF16)] + w_shapes,
        scratch_shapes=[pltpu.VMEM((4, s, tb), F32), pltpu.VMEM((4, s, tb), BF16)],
        compiler_params=_params(("arbitrary", "arbitrary")),
        name="fox",
    )(q, cq, k, ck, vt, *w_views)
    return att, [wb.reshape(w.shape) for wb, w in zip(w_bf16, ffn_weights)]


def _mix_mem_kernel(x_ref, conv_ref, att_ref, cb_ref, lg_ref, lb_ref, wo32_ref, gx_ref, wq32_ref, mk_ref,
                    mv_ref, wmo32_ref, o_ref, wo_ref, wq_ref, wmo_ref, *, n_sub):
    _cast_weights_once(2, (wo32_ref, wo_ref), (wq32_ref, wq_ref), (wmo32_ref, wmo_ref))
    cw = conv_ref.shape[-1]
    tm, d = x_ref.shape
    dh = d // MEM_HEADS

    def chain(rows):
        x1 = x_ref[rows, :] + _dot(att_ref[rows, :], wo_ref[cw:, :])
        conv = _conv_norm(conv_ref[rows, :], cb_ref[...], lg_ref[...], lb_ref[...])
        yield
        x1 = x1 + _dot(conv, wo_ref[0:cw, :])
        yield
        hx = _rms(x1, gx_ref[...]).astype(BF16)
        yield
        qm = (_dot(hx, wq_ref[...]) * (1.0 / math.sqrt(dh))).astype(BF16)
        yield
        outs = []
        for h in range(MEM_HEADS):
            s = _dot_nt(qm[:, h * dh:(h + 1) * dh], mk_ref[:, h * dh:(h + 1) * dh])
            yield
            e = jnp.exp(s - jnp.max(s, axis=-1, keepdims=True))
            p = e / jnp.sum(e, axis=-1, keepdims=True)
            yield
            outs.append(_dot(p.astype(BF16), mv_ref[:, h * dh:(h + 1) * dh]).astype(BF16))
            yield
        o = jnp.concatenate(outs, axis=-1)
        o_ref[rows, :] = x1 + _dot(o, wmo_ref[...])

    sub = tm // n_sub
    _run_staggered(chain(slice(i * sub, (i + 1) * sub)) for i in range(n_sub))


def _mix_mem(x, conv_out, att_out, conv_b, ln_g, ln_b, w_out, g_x, w_mq, mk, mv, w_mo, *, tm, n_sub):
    b, s, d = x.shape
    cw = conv_out.shape[-1]
    fw = att_out.shape[-1]
    m = mk.shape[1]
    row = lambda i, j: (i, j, 0)
    const = lambda i, j: (0, 0)
    first = lambda i, j: (0, 0, 0)
    batch = lambda i, j: (i, 0, 0)
    return pl.pallas_call(
        functools.partial(_mix_mem_kernel, n_sub=n_sub),
        grid=(b, s // tm),
        in_specs=[
            pl.BlockSpec((None, tm, d), row),
            pl.BlockSpec((None, tm, cw), row),
            pl.BlockSpec((None, tm, fw), row),
            pl.BlockSpec((1, cw), const),
            pl.BlockSpec((1, cw), const),
            pl.BlockSpec((1, cw), const),
            pl.BlockSpec((None, cw + fw, d), first),
            pl.BlockSpec((1, d), const),
            pl.BlockSpec((None, d, d), first),
            pl.BlockSpec((None, m, d), batch),
            pl.BlockSpec((None, m, d), batch),
            pl.BlockSpec((None, d, d), first),
        ],
        out_specs=pl.BlockSpec((None, tm, d), row),
        out_shape=jax.ShapeDtypeStruct((b, s, d), F32),
        scratch_shapes=[pltpu.VMEM((cw + fw, d), BF16), pltpu.VMEM((d, d), BF16), pltpu.VMEM((d, d), BF16)],
        compiler_params=_params(("arbitrary", "arbitrary")),
        name="mix_mem",
    )(x, conv_out, att_out, conv_b, ln_g, ln_b, w_out, g_x, w_mq, mk, mv, w_mo)


def _ffn_kernel(x_ref, gf_ref, wgu_ref, wd_ref, gl_ref, o_ref, *, n_sub):
    tm = x_ref.shape[0]
    dff = wd_ref.shape[0]

    def chain(rows):
        x2 = x_ref[rows, :]
        hf = _rms(x2, gf_ref[...]).astype(BF16)
        yield
        g = _dot(hf, wgu_ref[:, 0:dff])
        yield
        u = _dot(hf, wgu_ref[:, dff:2 * dff])
        act = (g * jax.nn.sigmoid(g) * u).astype(BF16)
        yield
        x3 = x2 + _dot(act, wd_ref[...])
        yield
        o_ref[rows, :] = _rms(x3, gl_ref[...])

    sub = tm // n_sub
    _run_staggered(chain(slice(i * sub, (i + 1) * sub)) for i in range(n_sub))


def _ffn(x2, g_ffn, w_gu, w_down, g_final, *, tm, n_sub):
    t, d = x2.shape
    dff = w_down.shape[0]
    assert w_gu.shape == (d, 2 * dff) and dff % LANES == 0
    row = lambda i: (i, 0)
    const = lambda i: (0, 0)
    return pl.pallas_call(
        functools.partial(_ffn_kernel, n_sub=n_sub),
        grid=(t // tm,),
        in_specs=[
            pl.BlockSpec((tm, d), row),
            pl.BlockSpec((1, d), const),
            pl.BlockSpec((d, 2 * dff), const),
            pl.BlockSpec((dff, d), const),
            pl.BlockSpec((1, d), const),
        ],
        out_specs=pl.BlockSpec((tm, d), row),
        out_shape=jax.ShapeDtypeStruct((t, d), F32),
        compiler_params=_params(("parallel",)),
        name="ffn",
    )(x2, g_ffn, w_gu, w_down, g_final)


def _tile(n, want):
    t = min(n, want)
    assert n % t == 0, (n, t)
    return t


def kernel(x, mem, g_mix, w_in, b_f, conv_w, conv_b, ln_g, ln_b, w_out, g_x, g_mem, w_mq, w_mkv,
           w_mo, g_ffn, w_gu, w_down, g_final):
    b, s, d = x.shape
    cw = conv_w.shape[-1]
    fw = FOX_HEADS * FOX_HEAD_DIM
    n_main = 2 * cw + 2 * fw
    assert w_in.shape[0] == 1, "single-layer block only"
    assert w_in.shape[-1] == n_main + fw + FOX_HEADS and conv_w.shape[1] == CONV_K
    assert fw % LANES == 0 and FOX_HEADS * 8 <= LANES

    mk, mv = _mem_kv(mem, g_mem.reshape(1, d), w_mkv)

    bf = jnp.pad(jnp.repeat(b_f[0], 8), (0, LANES - FOX_HEADS * 8)).reshape(1, LANES)

    w_in_t = jnp.swapaxes(w_in, 1, 2)
    conv_taps = jnp.broadcast_to(conv_w[0][:, None, :], (CONV_K, 8, cw))
    conv_pre, q, k, vt, cq, ck = _in_proj(x, g_mix[0].reshape(1, d), w_in_t, bf, conv_taps,
                                          tm=_tile(s, 512), n_main=n_main, cw=cw, fw=fw, conv_rows=16)
    att_out, (w_gu_bf16, w_down_bf16) = _fox(q, k, vt, cq, ck, [w_gu[0], w_down[0]], tb=_tile(s, 256),
                                             pps=4)
    tm_mix = _tile(s, 1024)
    x2 = _mix_mem(x, conv_pre, att_out, conv_b[0].reshape(1, cw), ln_g[0].reshape(1, cw),
                  ln_b[0].reshape(1, cw), w_out, g_x[0].reshape(1, d), w_mq, mk, mv, w_mo,
                  tm=tm_mix, n_sub=tm_mix // 256)
    y = _ffn(x2.reshape(b * s, d), g_ffn[0].reshape(1, d), w_gu_bf16, w_down_bf16,
             g_final.reshape(1, d), tm=_tile(b * s, 1024), n_sub=4)
    return y.reshape(b, s, d)
```
